```python
import jax, jax.numpy as jnp
from jax import lax
import numpy as np

D_MODEL = 1024
BATCH = 8
SEQ = 2048
DEPTH = 1

GRID_W = 64
CTX_LEN = 256
EPS = 1e-6
D_RNN = 1024
LRU_HEADS = 8
LRU_BW = D_RNN // LRU_HEADS
CONV_W = 4
CONV_PAD_L = 2
LRU_C = 8.0
HG_HEADS = 8
HG_DK = 128
HG_DV = 128
HG_WIDTH = HG_HEADS * HG_DK
HG_VW = HG_HEADS * HG_DV
CHUNK = 32
SPLITS = (D_RNN, 2 * D_RNN, 2 * D_RNN + HG_WIDTH, 2 * D_RNN + 2 * HG_WIDTH, 2 * D_RNN + 3 * HG_WIDTH,
          2 * D_RNN + 3 * HG_WIDTH + HG_VW, 2 * D_RNN + 3 * HG_WIDTH + 2 * HG_VW,
          2 * D_RNN + 3 * HG_WIDTH + 2 * HG_VW + D_MODEL)
IN_WIDTH = 2 * D_RNN + 3 * HG_WIDTH + 2 * HG_VW + 2 * D_MODEL
N_GROUPS = 4
EXP_PER_GROUP = 8
N_EXPERTS = N_GROUPS * EXP_PER_GROUP
TOP_K = 2
D_EXPERT = 512

kernel_name = "hybrid_rglru_hgrn2_hmoe_dit_block"

F32 = jnp.float32


def rmsnorm(x, g):
    xf = x.astype(F32)
    y = xf * lax.rsqrt(jnp.mean(xf * xf, axis=-1, keepdims=True) + EPS)
    return (y * g.astype(F32)).astype(x.dtype)


def modulate(h, shift, scale):
    return h * (1.0 + scale) + shift


def to_colmajor(t, rows):
    b, L, ch = t.shape
    return t.reshape(b, rows, GRID_W, ch).swapaxes(1, 2).reshape(b, L, ch)


def from_colmajor(t, rows):
    b, L, ch = t.shape
    return t.reshape(b, GRID_W, rows, ch).swapaxes(1, 2).reshape(b, L, ch)


def dwconv_centred(u, w, b):
    L = u.shape[1]
    up = jnp.pad(u, ((0, 0), (CONV_PAD_L, CONV_W - 1 - CONV_PAD_L), (0, 0)))
    return sum(up[:, k:k + L] * w[k] for k in range(CONV_W)) + b


def rglru_coeffs(u, wa, ba, wx, bx, lam):
    bsz, L, _ = u.shape
    uf = u.astype(F32)
    ub = uf.reshape(bsz, L, LRU_HEADS, LRU_BW)
    r = jax.nn.sigmoid(jnp.einsum('blhi,hij->blhj', ub, wa.astype(F32)).reshape(bsz, L, D_RNN) + ba)
    i = jax.nn.sigmoid(jnp.einsum('blhi,hij->blhj', ub, wx.astype(F32)).reshape(bsz, L, D_RNN) + bx)
    log_a = -LRU_C * r * jax.nn.softplus(-lam.astype(F32))
    a = jnp.exp(log_a)
    b = jnp.sqrt(-jnp.expm1(2.0 * log_a)) * (i * uf)
    return a, b


def linear_scan(a, b, h0, reverse):
    idx = -1 if reverse else 0
    b = b.at[:, idx].add(a[:, idx] * h0.astype(F32))

    def combine(l, r):
        return l[0] * r[0], r[0] * l[1] + r[1]

    _, h = lax.associative_scan(combine, (a, b), reverse=reverse, axis=1)
    return h, h[:, idx]


def hgrn2_chunk_scan(q, k, v, logf, s0, with_output):
    bsz, nh, L, _ = q.shape
    n = L // CHUNK

    def blocks(t):
        return t.reshape(bsz, nh, n, CHUNK, t.shape[-1]).transpose(2, 0, 1, 3, 4)

    lc = jnp.cumsum(blocks(logf), axis=3)
    mask = jnp.tril(jnp.ones((CHUNK, CHUNK), dtype=bool))

    def step(s, inp):
        qc, kc, vc, lcc = inp
        ltot = lcc[:, :, -1:, :]
        s_new = jnp.exp(lcc[:, :, -1, :])[..., None] * s + jnp.einsum('bhck,bhcv->bhkv', kc * jnp.exp(ltot - lcc), vc)
        if not with_output:
            return s_new, None
        lmid = lcc[:, :, CHUNK // 2:CHUNK // 2 + 1, :]
        o_inter = jnp.einsum('bhck,bhkv->bhcv', qc * jnp.exp(lcc), s)
        att = jnp.einsum('bhck,bhsk->bhcs', qc * jnp.exp(lcc - lmid), kc * jnp.exp(lmid - lcc))
        att = jnp.where(mask, att, 0.0)
        return s_new, o_inter + jnp.einsum('bhcs,bhsv->bhcv', att, vc)

    s, o = lax.scan(step, s0, (blocks(q), blocks(k), blocks(v), lc))
    if with_output:
        o = o.transpose(1, 2, 0, 3, 4).reshape(bsz, nh, L, -1)
    return o, s


def hgrn2_direction(q, f, v, lb, s0, reverse, with_output):
    bsz, L, _ = q.shape
    g = lb + (1.0 - lb) * jax.nn.sigmoid(f.astype(F32))

    def heads(t):
        t = t.astype(F32).reshape(bsz, L, HG_HEADS, -1).transpose(0, 2, 1, 3)
        return jnp.flip(t, axis=2) if reverse else t

    o, s = hgrn2_chunk_scan(heads(jax.nn.silu(q.astype(F32))), heads(1.0 - g), heads(v), heads(jnp.log(g)),
                            s0, with_output)
    if with_output and reverse:
        o = jnp.flip(o, axis=2)
    return o, s


def head_rmsnorm(o, g):
    o = o * lax.rsqrt(jnp.mean(o * o, axis=-1, keepdims=True) + EPS) * g.astype(F32).reshape(HG_HEADS, 1, HG_DV)
    b, h, L, dv = o.shape
    return o.transpose(0, 2, 1, 3).reshape(b, L, h * dv)


def branch_merge(ya, ay, yb, og, ga, gb, w_proj_a, w_proj_b, w_out):
    dt = ay.dtype
    pa = (ya * jax.nn.gelu(ay.astype(F32))).astype(dt) @ w_proj_a
    pb = (yb * jax.nn.silu(og.astype(F32))).astype(dt) @ w_proj_b
    return (jax.nn.sigmoid(ga) * pa + jax.nn.sigmoid(gb) * pb) @ w_out


def token_mixer(hx, hc, rows, lb, w_in, conv_w, conv_b, lru_wa, lru_ba, lru_wx, lru_bx, lru_lam,
                hg_norm_g, w_proj_a, w_proj_b, w_out, need_ctx):
    bsz = hx.shape[0]
    ax, ayx, qx, ffx, fbx, vx, ogx, gax, gbx = jnp.split(hx @ w_in, SPLITS, axis=-1)
    ac, ayc, qc, ffc, fbc, vc, ogc, gac, gbc = jnp.split(hc @ w_in, SPLITS, axis=-1)

    ux = dwconv_centred(ax, conv_w, conv_b)
    uc = dwconv_centred(ac, conv_w, conv_b)
    h0 = jnp.zeros((bsz, D_RNN), F32)
    ya_x, ya_c = 0.0, 0.0
    for d, rev in enumerate((False, True)):
        a_c, b_c = rglru_coeffs(uc, lru_wa[d], lru_ba[d], lru_wx[d], lru_bx[d], lru_lam[d])
        h_c, s_c = linear_scan(a_c, b_c, h0, rev)
        a_x, b_x = rglru_coeffs(ux, lru_wa[d], lru_ba[d], lru_wx[d], lru_bx[d], lru_lam[d])
        h_x, _ = linear_scan(a_x, b_x, s_c, rev)
        ya_x = ya_x + h_x
        if need_ctx:
            ya_c = ya_c + h_c

    qx_cm, ffx_cm, fbx_cm, vx_cm = (to_colmajor(t, rows) for t in (qx, ffx, fbx, vx))
    s0 = jnp.zeros((bsz, HG_HEADS, HG_DK, HG_DV), F32)
    yb_x, yb_c = 0.0, 0.0
    for rev, f_x, f_c in ((False, ffx_cm, ffc), (True, fbx_cm, fbc)):
        o_c, s_c = hgrn2_direction(qc, f_c, vc, lb, s0, rev, need_ctx)
        o_x, _ = hgrn2_direction(qx_cm, f_x, vx_cm, lb, s_c, rev, True)
        yb_x = yb_x + o_x
        if need_ctx:
            yb_c = yb_c + o_c
    yb_x = from_colmajor(head_rmsnorm(yb_x, hg_norm_g), rows)

    out_x = branch_merge(ya_x, ayx, yb_x, ogx, gax, gbx, w_proj_a, w_proj_b, w_out)
    out_c = None
    if need_ctx:
        out_c = branch_merge(ya_c, ayc, head_rmsnorm(yb_c, hg_norm_g), ogc, gac, gbc, w_proj_a, w_proj_b, w_out)
    return out_x, out_c


def expert_mlp(t, w1, w3, w2):
    return (jax.nn.silu(t @ w1) * (t @ w3)) @ w2


def hier_moe(h, w_rg, b_rg, w_re, b_re, w1, w3, w2):
    shp = h.shape
    t = h.reshape(-1, shp[-1])
    n_tok = t.shape[0]
    tf = t.astype(F32)
    p_grp = jax.nn.softmax(tf @ w_rg.astype(F32) + b_rg, axis=-1)
    pg, gi = lax.top_k(p_grp, 1)
    e_logit = (tf @ w_re.astype(F32) + b_re).reshape(n_tok, N_GROUPS, EXP_PER_GROUP)
    e_sel = jnp.einsum('tge,tg->te', e_logit, jax.nn.one_hot(gi[:, 0], N_GROUPS, dtype=F32))
    pe, ei = lax.top_k(jax.nn.softmax(e_sel, axis=-1), TOP_K)
    wts = pg * pe / jnp.sum(pe, axis=-1, keepdims=True)
    eid = gi * EXP_PER_GROUP + ei
    cw = jnp.einsum('tk,tkn->tn', wts, jax.nn.one_hot(eid, N_EXPERTS, dtype=F32))
    y = jnp.zeros_like(t)
    for e in range(N_EXPERTS):
        y = y + (cw[:, e:e + 1] * expert_mlp(t, w1[e], w3[e], w2[e])).astype(t.dtype)
    return y.reshape(shp)


def setup_inputs(seed: int = 0) -> dict:
    key = jax.random.key(seed)
    ks = jax.random.split(key, 32)

    def nrm(k, shape, scale):
        return jax.random.normal(k, shape, F32) * scale

    u = jax.random.uniform(ks[14], (DEPTH, 2, D_RNN), F32, 0.9, 0.999)
    a0 = u ** (1.0 / LRU_C)
    return {
        "x": nrm(ks[0], (BATCH, SEQ, D_MODEL), 1.0),
        "c": nrm(ks[1], (BATCH, D_MODEL), 1.0),
        "ctx": nrm(ks[2], (BATCH, CTX_LEN, D_MODEL), 1.0),
        "c_ctx": nrm(ks[3], (D_MODEL,), 1.0),
        "w_mod": nrm(ks[4], (DEPTH, D_MODEL, 6 * D_MODEL), 0.5 * D_MODEL ** -0.5),
        "b_mod": nrm(ks[5], (DEPTH, 6 * D_MODEL), 0.02),
        "norm1_g": 1.0 + nrm(ks[6], (DEPTH, D_MODEL), 0.05),
        "w_in": nrm(ks[7], (DEPTH, D_MODEL, IN_WIDTH), D_MODEL ** -0.5),
        "conv_w": nrm(ks[8], (DEPTH, CONV_W, D_RNN), CONV_W ** -0.5),
        "conv_b": nrm(ks[9], (DEPTH, D_RNN), 0.02),
        "lru_wa": nrm(ks[10], (DEPTH, 2, LRU_HEADS, LRU_BW, LRU_BW), LRU_BW ** -0.5),
        "lru_ba": nrm(ks[11], (DEPTH, 2, D_RNN), 0.02),
        "lru_wx": nrm(ks[12], (DEPTH, 2, LRU_HEADS, LRU_BW, LRU_BW), LRU_BW ** -0.5),
        "lru_bx": nrm(ks[13], (DEPTH, 2, D_RNN), 0.02),
        "lru_lam": jnp.log(a0) - jnp.log1p(-a0),
        "hg_lb": nrm(ks[15], (DEPTH + 1, HG_WIDTH), 0.1),
        "hg_norm_g": 1.0 + nrm(ks[16], (DEPTH, HG_VW), 0.05),
        "w_proj_a": nrm(ks[17], (DEPTH, D_RNN, D_MODEL), D_RNN ** -0.5),
        "w_proj_b": nrm(ks[18], (DEPTH, HG_VW, D_MODEL), HG_VW ** -0.5),
        "w_out": nrm(ks[19], (DEPTH, D_MODEL, D_MODEL), D_MODEL ** -0.5),
        "norm2_g": 1.0 + nrm(ks[20], (DEPTH, D_MODEL), 0.05),
        "w_rg": nrm(ks[21], (DEPTH, D_MODEL, N_GROUPS), D_MODEL ** -0.5),
        "b_rg": nrm(ks[22], (DEPTH, N_GROUPS), 0.01),
        "w_re": nrm(ks[23], (DEPTH, D_MODEL, N_EXPERTS), D_MODEL ** -0.5),
        "b_re": nrm(ks[24], (DEPTH, N_EXPERTS), 0.01),
        "w1": nrm(ks[25], (DEPTH, N_EXPERTS, D_MODEL, D_EXPERT), D_MODEL ** -0.5),
        "w3": nrm(ks[26], (DEPTH, N_EXPERTS, D_MODEL, D_EXPERT), D_MODEL ** -0.5),
        "w2": nrm(ks[27], (DEPTH, N_EXPERTS, D_EXPERT, D_MODEL), D_EXPERT ** -0.5),
        "final_g": 1.0 + nrm(ks[28], (D_MODEL,), 0.05),
    }


def reference(x, c, ctx, c_ctx, w_mod, b_mod, norm1_g, w_in, conv_w, conv_b, lru_wa, lru_ba, lru_wx, lru_bx,
              lru_lam, hg_lb, hg_norm_g, w_proj_a, w_proj_b, w_out, norm2_g, w_rg, b_rg, w_re, b_re,
              w1, w3, w2, final_g):
    rows = x.shape[1] // GRID_W
    lb_all = jnp.cumsum(jax.nn.softmax(hg_lb.astype(F32), axis=0), axis=0)
    sc = jax.nn.silu(c)
    scc = jax.nn.silu(c_ctx)
    for l in range(DEPTH):
        need_ctx = l < DEPTH - 1
        mx = jnp.split((sc @ w_mod[l] + b_mod[l])[:, None, :], 6, axis=-1)
        mc = jnp.split(scc @ w_mod[l] + b_mod[l], 6, axis=-1)
        hx = modulate(rmsnorm(x, norm1_g[l]), mx[0], mx[1])
        hc = modulate(rmsnorm(ctx, norm1_g[l]), mc[0], mc[1])
        ox, oc = token_mixer(hx, hc, rows, lb_all[l], w_in[l], conv_w[l], conv_b[l], lru_wa[l], lru_ba[l],
                             lru_wx[l], lru_bx[l], lru_lam[l], hg_norm_g[l], w_proj_a[l], w_proj_b[l],
                             w_out[l], need_ctx)
        x = x + mx[2] * ox
        hx2 = modulate(rmsnorm(x, norm2_g[l]), mx[3], mx[4])
        x = x + mx[5] * hier_moe(hx2, w_rg[l], b_rg[l], w_re[l], b_re[l], w1[l], w3[l], w2[l])
        if need_ctx:
            ctx = ctx + mc[2] * oc
            hc2 = modulate(rmsnorm(ctx, norm2_g[l]), mc[3], mc[4])
            ctx = ctx + mc[5] * hier_moe(hc2, w_rg[l], b_rg[l], w_re[l], b_re[l], w1[l], w3[l], w2[l])
    return rmsnorm(x, final_g)
```

```python
import functools

import jax
import jax.numpy as jnp
from jax import lax
from jax.experimental import pallas as pl
from jax.experimental.pallas import tpu as pltpu

F32 = jnp.float32
BF16 = jnp.bfloat16
HIGHEST = lax.Precision.HIGHEST

EPS = 1e-6
GRID_W = 64
CHUNK = 32
LRU_C = 8.0
CONV_W = 4
N_HEADS = 8
HEAD_DIM = 128
N_GROUPS = 4
EXP_PER_GROUP = 8
N_EXPERTS = N_GROUPS * EXP_PER_GROUP
LANES = 128
SUBLANES = 8
VMEM_LIMIT = 56 * 1024 * 1024


def _params(*sem):
    return pltpu.CompilerParams(dimension_semantics=sem, vmem_limit_bytes=VMEM_LIMIT)


def _silu(x):
    return x * jax.nn.sigmoid(x)


def _bdot(a, b):
    return jnp.dot(a.astype(BF16), b.astype(BF16), preferred_element_type=F32)


def _mod_kernel(c_ref, w_ref, b_ref, o_ref):
    o_ref[...] = jnp.dot(_silu(c_ref[...]), w_ref[...], precision=HIGHEST,
                         preferred_element_type=F32) + b_ref[...]


def _mod_call(cc, w_mod, b_mod):
    rows, d = cc.shape
    n = w_mod.shape[1]
    tn = 1024
    return pl.pallas_call(
        _mod_kernel,
        grid=(n // tn,),
        in_specs=[pl.BlockSpec((rows, d), lambda j: (0, 0)),
                  pl.BlockSpec((d, tn), lambda j: (0, j)),
                  pl.BlockSpec((1, tn), lambda j: (0, j))],
        out_specs=pl.BlockSpec((rows, tn), lambda j: (0, j)),
        out_shape=jax.ShapeDtypeStruct((rows, n), F32),
        compiler_params=_params("arbitrary"),
        name="mod",
    )(cc, w_mod, b_mod.reshape(1, n))


def _norm_mod(x, g, shift, scale):
    y = x * lax.rsqrt(jnp.mean(x * x, axis=-1, keepdims=True) + EPS) * g
    return y * (1.0 + scale) + shift


def _norm_kernel(x_ref, g_ref, shift_ref, scale_ref, o_ref):
    o_ref[0] = _norm_mod(x_ref[0], g_ref[...], shift_ref[0], scale_ref[0]).astype(o_ref.dtype)


def _norm_call(x, g, shift, scale):
    b, l, d = x.shape
    tl = min(l, 512)
    vec = pl.BlockSpec((1, 1, d), lambda i, j: (i, 0, 0))
    return pl.pallas_call(
        _norm_kernel,
        grid=(b, l // tl),
        in_specs=[pl.BlockSpec((1, tl, d), lambda i, j: (i, j, 0)),
                  pl.BlockSpec((1, d), lambda i, j: (0, 0)), vec, vec],
        out_specs=pl.BlockSpec((1, tl, d), lambda i, j: (i, j, 0)),
        out_shape=jax.ShapeDtypeStruct((b, l, d), BF16),
        compiler_params=_params("arbitrary", "arbitrary"),
        name="norm",
    )(x, g.reshape(1, d), shift, scale)


def _proj_kernel(a_ref, w_ref, o_ref):
    o_ref[0] = jnp.dot(a_ref[...], w_ref[...].astype(BF16), preferred_element_type=F32)


def _proj_call(a, w, col_of, n_out, name):
    t, k = a.shape
    tn = 1024
    tm = min(t, 2048)
    return pl.pallas_call(
        _proj_kernel,
        grid=(t // tm, n_out),
        in_specs=[pl.BlockSpec((tm, k), lambda i, j: (i, 0)),
                  pl.BlockSpec((k, tn), lambda i, j: (0, col_of(j)))],
        out_specs=pl.BlockSpec((1, tm, tn), lambda i, j: (j, i, 0)),
        out_shape=jax.ShapeDtypeStruct((n_out, t, tn), F32),
        compiler_params=_params("arbitrary", "arbitrary"),
        name=name,
    )(a, w)


def _shift_rows(x, k):
    n = x.shape[0]
    row = lax.broadcasted_iota(jnp.int32, x.shape, 0)
    rolled = pltpu.roll(x, k % n, 0)
    keep = (row >= k) if k > 0 else (row < n + k)
    return jnp.where(keep, rolled, 0.0)


def _lru_kernel(ax_ref, axc_ref, cw_ref, cb_ref, wa_ref, wx_ref, ba_ref, bx_ref, lam_ref,
                o_ref, a_s, b_s):
    seq, width = ax_ref.shape[1], ax_ref.shape[2]
    seq_c = axc_ref.shape[1]
    heads = width // HEAD_DIM

    def conv(x):
        acc = _shift_rows(x, 2) * cw_ref[0:1, :]
        acc = acc + _shift_rows(x, 1) * cw_ref[1:2, :]
        acc = acc + x * cw_ref[2:3, :]
        acc = acc + _shift_rows(x, -1) * cw_ref[3:4, :]
        return acc + cb_ref[...]

    def coeffs(u, d, h):
        hs = slice(h * HEAD_DIM, (h + 1) * HEAD_DIM)
        ub = u.astype(BF16)
        r = jax.nn.sigmoid(_bdot(ub, wa_ref[d, h]) + ba_ref[d:d + 1, hs])
        i = jax.nn.sigmoid(_bdot(ub, wx_ref[d, h]) + bx_ref[d:d + 1, hs])
        log_a = -LRU_C * r * jax.nn.softplus(-lam_ref[d:d + 1, hs])
        a = jnp.exp(log_a)
        return a, jnp.sqrt(1.0 - a * a) * (i * u)

    u_c = conv(axc_ref[0])
    u_x = conv(ax_ref[0])
    n_blk = seq // SUBLANES
    row8 = lax.broadcasted_iota(jnp.int32, (SUBLANES, width), 0)

    for d, reverse in enumerate((False, True)):
        edge = u_c[seq_c - SUBLANES:] if reverse else u_c[:SUBLANES]
        pick = SUBLANES - 1 if reverse else 0
        h0 = jnp.concatenate([coeffs(edge[:, h * HEAD_DIM:(h + 1) * HEAD_DIM], d, h)[1][pick:pick + 1]
                              for h in range(heads)], axis=1)
        for h in range(heads):
            hs = slice(h * HEAD_DIM, (h + 1) * HEAD_DIM)
            a, b = coeffs(u_x[:, hs], d, h)
            a_s[:, hs] = a
            b_s[:, hs] = b

        def body(i, h_prev, reverse=reverse):
            blk = (n_blk - 1 - i) if reverse else i
            rows = pl.ds(pl.multiple_of(blk * SUBLANES, SUBLANES), SUBLANES)
            a = a_s[rows, :]
            b = b_s[rows, :]
            for s in (1, 2, 4):
                if reverse:
                    keep = row8 < SUBLANES - s
                    sh = SUBLANES - s
                else:
                    keep = row8 >= s
                    sh = s
                a_sh = jnp.where(keep, pltpu.roll(a, sh, 0), 1.0)
                b_sh = jnp.where(keep, pltpu.roll(b, sh, 0), 0.0)
                b = a * b_sh + b
                a = a * a_sh
            hh = b + a * h_prev
            if reverse:
                o_ref[0, rows, :] = o_ref[0, rows, :] + hh
                return hh[0:1, :]
            o_ref[0, rows, :] = hh
            return hh[SUBLANES - 1:SUBLANES, :]

        lax.fori_loop(0, n_blk, body, h0, unroll=4)


def _lru_call(z, zc, conv_w, conv_b, wa, wx, ba, bx, lam, bsz):
    t = z.shape[1]
    seq = t // bsz
    seq_c = zc.shape[1] // bsz
    width = 2 * HEAD_DIM
    n_w = (N_HEADS * HEAD_DIM) // width
    z4 = z.reshape(z.shape[0], bsz, seq, z.shape[2])
    zc4 = zc.reshape(zc.shape[0], bsz, seq_c, zc.shape[2])
    vec2 = pl.BlockSpec((2, width), lambda b, w: (0, w))
    wspec = pl.BlockSpec((2, width // HEAD_DIM, HEAD_DIM, HEAD_DIM), lambda b, w: (0, w, 0, 0))

    def kern(ax_ref, axc_ref, *rest):
        _lru_kernel(ax_ref.at[0], axc_ref.at[0], *rest)

    return pl.pallas_call(
        kern,
        grid=(bsz, n_w),
        in_specs=[pl.BlockSpec((1, 1, seq, width), lambda b, w: (0, b, 0, w)),
                  pl.BlockSpec((1, 1, seq_c, width), lambda b, w: (0, b, 0, w)),
                  pl.BlockSpec((CONV_W, width), lambda b, w: (0, w)),
                  pl.BlockSpec((1, width), lambda b, w: (0, w)),
                  wspec, wspec, vec2, vec2, vec2],
        out_specs=pl.BlockSpec((1, seq, width), lambda b, w: (b, 0, w)),
        out_shape=jax.ShapeDtypeStruct((bsz, seq, N_HEADS * HEAD_DIM), F32),
        scratch_shapes=[pltpu.VMEM((seq, width), F32), pltpu.VMEM((seq, width), F32)],
        compiler_params=_params("arbitrary", "arbitrary"),
        name="lru",
    )(z4, zc4, conv_w, conv_b.reshape(1, -1), wa, wx, ba, bx, lam)


def _split3(x):
    x1 = x.astype(BF16)
    r1 = x - x1.astype(F32)
    x2 = r1.astype(BF16)
    x3 = (r1 - x2.astype(F32)).astype(BF16)
    return x1, x2, x3


def _tri_cumsum(tri, x):
    x1, x2, x3 = _split3(x)
    dot = functools.partial(jnp.dot, preferred_element_type=F32)
    return dot(tri, x1) + dot(tri, x2) + dot(tri, x3)


def _hg_gates(f, lb):
    g = lb + (1.0 - lb) * jax.nn.sigmoid(f)
    return jnp.log(g), 1.0 - g


def _dot_tn(a, b):
    return lax.dot_general(a.astype(BF16), b.astype(BF16), (((0,), (0,)), ((), ())),
                           preferred_element_type=F32)


def _dot_nt(a, b):
    return lax.dot_general(a.astype(BF16), b.astype(BF16), (((1,), (1,)), ((), ())),
                           preferred_element_type=F32)


def _hgc_kernel(ff_ref, fb_ref, v_ref, lb_ref, sf_ref, sb_ref):
    n = ff_ref.shape[1]
    row = lax.broadcasted_iota(jnp.int32, (n, n), 0)
    col = lax.broadcasted_iota(jnp.int32, (n, n), 1)
    tri = (col <= row).astype(BF16)
    lb = lb_ref[...]
    v = v_ref[0]
    lg, k = _hg_gates(ff_ref[0], lb)
    lc = _tri_cumsum(tri, lg)
    kf = k * jnp.exp(lc[n - 1:n, :] - lc)
    lg, k = _hg_gates(fb_ref[0], lb)
    lc = _tri_cumsum(tri, lg)
    kb = k * jnp.exp(lc - lg)
    for h in range(N_HEADS):
        hs = slice(h * HEAD_DIM, (h + 1) * HEAD_DIM)
        sf_ref[0, h] = _dot_tn(v[:, hs], kf[:, hs])
        sb_ref[0, h] = _dot_tn(v[:, hs], kb[:, hs])


def _hgc_call(zc, lb, bsz):
    seq_c = zc.shape[1] // bsz
    width = zc.shape[2]
    zc4 = zc.reshape(zc.shape[0], bsz, seq_c, width)

    def slab(j):
        return pl.BlockSpec((1, 1, seq_c, width), lambda b: (j, b, 0, 0))

    def kern(ff_ref, fb_ref, v_ref, lb_ref, sf_ref, sb_ref):
        _hgc_kernel(ff_ref.at[0], fb_ref.at[0], v_ref.at[0], lb_ref, sf_ref, sb_ref)

    sspec = pl.BlockSpec((1, N_HEADS, HEAD_DIM, HEAD_DIM), lambda b: (b, 0, 0, 0))
    sshape = jax.ShapeDtypeStruct((bsz, N_HEADS, HEAD_DIM, HEAD_DIM), F32)
    return pl.pallas_call(
        kern,
        grid=(bsz,),
        in_specs=[slab(1), slab(2), slab(3), pl.BlockSpec((1, width), lambda b: (0, 0))],
        out_specs=[sspec, sspec],
        out_shape=[sshape, sshape],
        compiler_params=_params("arbitrary"),
        name="hgc",
    )(zc4, zc4, zc4, lb.reshape(1, width))


def _hg_kernel(q_ref, f_ref, v_ref, lb_ref, s0_ref, o_ref, st_s, *, reverse, n_chunks):
    width = N_HEADS * HEAD_DIM

    @pl.when(pl.program_id(1) == 0)
    def _():
        st_s[...] = s0_ref[0]

    row = lax.broadcasted_iota(jnp.int32, (CHUNK, CHUNK), 0)
    col = lax.broadcasted_iota(jnp.int32, (CHUNK, CHUNK), 1)
    causal = (col >= row) if reverse else (col <= row)
    tri = causal.astype(BF16)
    last, mid = (0, CHUNK - 1 - CHUNK // 2) if reverse else (CHUNK - 1, CHUNK // 2)
    lb = lb_ref[...]

    for cc in range(n_chunks):
        c = (n_chunks - 1 - cc) if reverse else cc
        cs = slice(c * width, (c + 1) * width)
        q = _silu(q_ref[0, :, cs])
        v = v_ref[0, :, cs]
        lg, k = _hg_gates(f_ref[0, :, cs], lb)
        lc = _tri_cumsum(tri, lg)
        ltot = lc[last:last + 1, :]
        lmid = lc[mid:mid + 1, :]
        e_tot = jnp.exp(ltot)
        q_in = q * jnp.exp(lc)
        q_at = q * jnp.exp(lc - lmid)
        k_at = k * jnp.exp(lmid - lc)
        k_st = k * jnp.exp(ltot - lc)
        for h in range(N_HEADS):
            hs = slice(h * HEAD_DIM, (h + 1) * HEAD_DIM)
            st = st_s[h]
            att = jnp.where(causal, _dot_nt(q_at[:, hs], k_at[:, hs]), 0.0)
            o = _dot_nt(q_in[:, hs], st) + _bdot(att, v[:, hs])
            o_ref[0, :, c * width + h * HEAD_DIM:c * width + (h + 1) * HEAD_DIM] = o
            st_s[h] = st * e_tot[:, hs] + _dot_tn(v[:, hs], k_st[:, hs])


def _hg_call(z, f_slab, lb, s0, bsz, reverse):
    t, width = z.shape[1], z.shape[2]
    seq = t // bsz
    rows = seq // GRID_W
    assert rows == CHUNK
    n_chunks = 8
    n_grp = GRID_W // n_chunks
    z4 = z.reshape(z.shape[0], bsz, rows, GRID_W * width)

    def grp(g):
        return (n_grp - 1 - g) if reverse else g

    def slab(j):
        return pl.BlockSpec((1, 1, rows, n_chunks * width), lambda b, g: (j, b, 0, grp(g)))

    def kern(q_ref, f_ref, v_ref, lb_ref, s0_ref, o_ref, st_s):
        _hg_kernel(q_ref.at[0], f_ref.at[0], v_ref.at[0], lb_ref, s0_ref, o_ref, st_s,
                   reverse=reverse, n_chunks=n_chunks)

    o = pl.pallas_call(
        kern,
        grid=(bsz, n_grp),
        in_specs=[slab(2), slab(f_slab), slab(5),
                  pl.BlockSpec((1, width), lambda b, g: (0, 0)),
                  pl.BlockSpec((1, N_HEADS, HEAD_DIM, HEAD_DIM), lambda b, g: (b, 0, 0, 0))],
        out_specs=pl.BlockSpec((1, rows, n_chunks * width), lambda b, g: (b, 0, grp(g))),
        out_shape=jax.ShapeDtypeStruct((bsz, rows, GRID_W * width), F32),
        scratch_shapes=[pltpu.VMEM((N_HEADS, HEAD_DIM, HEAD_DIM), F32)],
        compiler_params=_params("arbitrary", "arbitrary"),
        name="hg_bwd" if reverse else "hg_fwd",
    )(z4, z4, z4, lb.reshape(1, width), s0)
    return o.reshape(bsz, seq, width)


def _merge_kernel(x_ref, ya_ref, ay_ref, of_ref, ob_ref, og_ref, ga_ref, gb_ref,
                  gate_ref, shift_ref, scale_ref, hgn_ref, n2_ref,
                  wpa_ref, wpb_ref, wo_ref, wr_ref, br_ref,
                  x1_ref, h2_ref, cw_ref):
    tm = x_ref.shape[1]
    pa = _bdot(ya_ref[0] * jax.nn.gelu(ay_ref[0, 0]), wpa_ref[...])
    yb = of_ref[0] + ob_ref[0]
    parts = []
    for h in range(N_HEADS):
        hs = slice(h * HEAD_DIM, (h + 1) * HEAD_DIM)
        o = yb[:, hs]
        parts.append(o * lax.rsqrt(jnp.mean(o * o, axis=-1, keepdims=True) + EPS) * hgn_ref[:, hs])
    yb = jnp.concatenate(parts, axis=1)
    pb = _bdot(yb * _silu(og_ref[0, 0]), wpb_ref[...])
    merged = jax.nn.sigmoid(ga_ref[0, 0]) * pa + jax.nn.sigmoid(gb_ref[0, 0]) * pb
    x1 = x_ref[0] + gate_ref[0] * _bdot(merged, wo_ref[...])
    x1_ref[0] = x1
    h2 = _norm_mod(x1, n2_ref[...], shift_ref[0], scale_ref[0])
    h2_ref[0] = h2

    logit = jnp.dot(h2, wr_ref[...], precision=HIGHEST, preferred_element_type=F32) + br_ref[...]
    col = lax.broadcasted_iota(jnp.int32, (tm, LANES), 1)
    colf = col.astype(F32)
    neg = -jnp.inf

    def first_at(mask):
        return -jnp.max(jnp.where(mask, -colf, -1e9), axis=1, keepdims=True)

    is_g = (col >= N_EXPERTS) & (col < N_EXPERTS + N_GROUPS)
    lgm = jnp.where(is_g, logit, neg)
    mg = jnp.max(lgm, axis=1, keepdims=True)
    pg = 1.0 / jnp.sum(jnp.exp(lgm - mg), axis=1, keepdims=True)
    gi = first_at(is_g & (logit == mg)) - float(N_EXPERTS)
    is_e = (col < N_EXPERTS) & ((col // EXP_PER_GROUP).astype(F32) == gi)
    lem = jnp.where(is_e, logit, neg)
    ee = jnp.exp(lem - jnp.max(lem, axis=1, keepdims=True))
    pe = ee / jnp.sum(ee, axis=1, keepdims=True)
    p1 = jnp.max(jnp.where(is_e, pe, -1.0), axis=1, keepdims=True)
    i1 = first_at(is_e & (pe == p1))
    rest = is_e & (colf != i1)
    p2 = jnp.max(jnp.where(rest, pe, -1.0), axis=1, keepdims=True)
    i2 = first_at(rest & (pe == p2))
    den = p1 + p2
    cw_ref[0] = (jnp.where(colf == i1, pg * p1 / den, 0.0) + jnp.where(colf == i2, pg * p2 / den, 0.0))


def _merge_call(x, ya, z, o_f, o_b, gate, shift, scale, hg_norm_g, norm2_g,
                w_proj_a, w_proj_b, w_out, w_router, b_router):
    bsz, seq, d = x.shape
    tm = min(seq, 256)
    z4 = z.reshape(z.shape[0], bsz, seq, d)
    tok = pl.BlockSpec((1, tm, d), lambda b, i: (b, i, 0))

    def slab(j):
        return pl.BlockSpec((1, 1, tm, d), lambda b, i: (j, b, i, 0))

    vec = pl.BlockSpec((1, 1, d), lambda b, i: (b, 0, 0))
    row = pl.BlockSpec((1, d), lambda b, i: (0, 0))
    wsp = pl.BlockSpec((d, d), lambda b, i: (0, 0))
    return pl.pallas_call(
        _merge_kernel,
        grid=(bsz, seq // tm),
        in_specs=[tok, tok, slab(1), tok, tok, slab(6), slab(7), slab(8),
                  vec, vec, vec, row, row, wsp, wsp, wsp,
                  pl.BlockSpec((d, LANES), lambda b, i: (0, 0)),
                  pl.BlockSpec((1, LANES), lambda b, i: (0, 0))],
        out_specs=[tok, tok, pl.BlockSpec((1, tm, LANES), lambda b, i: (b, i, 0))],
        out_shape=[jax.ShapeDtypeStruct((bsz, seq, d), F32),
                   jax.ShapeDtypeStruct((bsz, seq, d), F32),
                   jax.ShapeDtypeStruct((bsz, seq, LANES), F32)],
        compiler_params=_params("arbitrary", "arbitrary"),
        name="merge",
    )(x, ya, z4, o_f, o_b, z4, z4, z4, gate, shift, scale,
      hg_norm_g.reshape(1, d), norm2_g.reshape(1, d),
      w_proj_a.astype(BF16), w_proj_b.astype(BF16), w_out.astype(BF16), w_router, b_router)


def _moe_kernel(h_ref, cw_ref, x1_ref, gate_ref, fg_ref, w1_ref, w3_ref, w2_ref, o_ref, acc_s):
    e = pl.program_id(1)

    @pl.when(e == 0)
    def _():
        acc_s[...] = jnp.zeros_like(acc_s)

    hb = h_ref[...].astype(BF16)
    up = _silu(_bdot(hb, w1_ref[0])) * _bdot(hb, w3_ref[0])
    y = _bdot(up, w2_ref[0])
    col = lax.broadcasted_iota(jnp.int32, cw_ref.shape, 1)
    c = jnp.sum(jnp.where(col == e, cw_ref[...], 0.0), axis=1, keepdims=True)
    acc_s[...] += c * y

    @pl.when(e == pl.num_programs(1) - 1)
    def _():
        x2 = x1_ref[...] + gate_ref[0] * acc_s[...]
        o_ref[...] = x2 * lax.rsqrt(jnp.mean(x2 * x2, axis=-1, keepdims=True) + EPS) * fg_ref[...]


def _moe_call(h2, cw, x1, gate, final_g, w1, w3, w2, seq):
    t, d = h2.shape
    tm = min(seq, 1024)
    per_b = seq // tm
    n_e, _, d_e = w1.shape
    tok = pl.BlockSpec((tm, d), lambda i, e: (i, 0))
    return pl.pallas_call(
        _moe_kernel,
        grid=(t // tm, n_e),
        in_specs=[tok, pl.BlockSpec((tm, LANES), lambda i, e: (i, 0)), tok,
                  pl.BlockSpec((1, 1, d), lambda i, e: (i // per_b, 0, 0)),
                  pl.BlockSpec((1, d), lambda i, e: (0, 0)),
                  pl.BlockSpec((1, d, d_e), lambda i, e: (e, 0, 0)),
                  pl.BlockSpec((1, d, d_e), lambda i, e: (e, 0, 0)),
                  pl.BlockSpec((1, d_e, d), lambda i, e: (e, 0, 0))],
        out_specs=tok,
        out_shape=jax.ShapeDtypeStruct((t, d), F32),
        scratch_shapes=[pltpu.VMEM((tm, d), F32)],
        compiler_params=_params("arbitrary", "arbitrary"),
        name="moe",
    )(h2, cw, x1, gate, final_g.reshape(1, d), w1, w3, w2)


def kernel(x, c, ctx, c_ctx, w_mod, b_mod, norm1_g, w_in, conv_w, conv_b, lru_wa, lru_ba, lru_wx, lru_bx,
           lru_lam, hg_lb, hg_norm_g, w_proj_a, w_proj_b, w_out, norm2_g, w_rg, b_rg, w_re, b_re,
           w1, w3, w2, final_g):
    bsz, seq, d = x.shape
    seq_c = ctx.shape[1]
    l = 0

    pad = (-(bsz + 1)) % SUBLANES
    cc = jnp.concatenate([c, c_ctx[None, :], jnp.zeros((pad, d), F32)], axis=0)
    mod = _mod_call(cc, w_mod[l], b_mod[l])
    mx = [mod[:bsz, None, k * d:(k + 1) * d] for k in range(6)]
    mc = [jnp.broadcast_to(mod[bsz:bsz + 1, None, k * d:(k + 1) * d], (bsz, 1, d)) for k in range(2)]
    lb = jax.nn.softmax(hg_lb.astype(F32), axis=0)[l]

    hx = _norm_call(x, norm1_g[l], mx[0], mx[1]).reshape(bsz * seq, d)
    hc = _norm_call(ctx, norm1_g[l], mc[0], mc[1]).reshape(bsz * seq_c, d)
    z = _proj_call(hx, w_in[l], lambda j: j, 9, "proj_x")
    zc = _proj_call(hc, w_in[l], lambda j: jnp.where(j > 0, j + 2, 0), 4, "proj_c")

    ya = _lru_call(z, zc, conv_w[l], conv_b[l], lru_wa[l], lru_wx[l], lru_ba[l], lru_bx[l], lru_lam[l], bsz)
    s_f, s_b = _hgc_call(zc, lb, bsz)
    o_f = _hg_call(z, 3, lb, s_f, bsz, reverse=False)
    o_b = _hg_call(z, 4, lb, s_b, bsz, reverse=True)

    w_router = jnp.zeros((d, LANES), F32).at[:, :N_EXPERTS].set(w_re[l])
    w_router = w_router.at[:, N_EXPERTS:N_EXPERTS + N_GROUPS].set(w_rg[l])
    b_router = jnp.zeros((1, LANES), F32).at[0, :N_EXPERTS].set(b_re[l])
    b_router = b_router.at[0, N_EXPERTS:N_EXPERTS + N_GROUPS].set(b_rg[l])
    x1, h2, cw = _merge_call(x, ya, z, o_f, o_b, mx[2], mx[3], mx[4], hg_norm_g[l], norm2_g[l],
                             w_proj_a[l], w_proj_b[l], w_out[l], w_router, b_router)

    out = _moe_call(h2.reshape(bsz * seq, d), cw.reshape(bsz * seq, LANES), x1.reshape(bsz * seq, d),
                    mx[5], final_g, w1[l], w3[l], w2[l], seq)
    return out.reshape(bsz, seq, d)
```

```python
import functools

import jax
import jax.numpy as jnp
from jax import lax
from jax.experimental import pallas as pl
from jax.experimental.pallas import tpu as pltpu

F32 = jnp.float32
BF16 = jnp.bfloat16
HIGHEST = lax.Precision.HIGHEST

EPS = 1e-6
GRID_W = 64
CHUNK = 32
LRU_C = 8.0
CONV_W = 4
N_HEADS = 8
HEAD_DIM = 128
N_GROUPS = 4
EXP_PER_GROUP = 8
N_EXPERTS = N_GROUPS * EXP_PER_GROUP
LANES = 128
SUBLANES = 8
VMEM_LIMIT = 56 * 1024 * 1024


def _params(*sem):
    return pltpu.CompilerParams(dimension_semantics=sem, vmem_limit_bytes=VMEM_LIMIT)


def _silu(x):
    return x * jax.nn.sigmoid(x)


def _bdot(a, b):
    return jnp.dot(a.astype(BF16), b.astype(BF16), preferred_element_type=F32)


def _mod_kernel(c_ref, w_ref, b_ref, o_ref):
    o_ref[...] = jnp.dot(_silu(c_ref[...]), w_ref[...], precision=HIGHEST,
                         preferred_element_type=F32) + b_ref[...]


def _mod_call(cc, w_mod, b_mod):
    rows, d = cc.shape
    n = w_mod.shape[1]
    tn = 1024
    return pl.pallas_call(
        _mod_kernel,
        grid=(n // tn,),
        in_specs=[pl.BlockSpec((rows, d), lambda j: (0, 0)),
                  pl.BlockSpec((d, tn), lambda j: (0, j)),
                  pl.BlockSpec((1, tn), lambda j: (0, j))],
        out_specs=pl.BlockSpec((rows, tn), lambda j: (0, j)),
        out_shape=jax.ShapeDtypeStruct((rows, n), F32),
        compiler_params=_params("arbitrary"),
        name="mod",
    )(cc, w_mod, b_mod.reshape(1, n))


def _norm_mod(x, g, shift, scale):
    y = x * lax.rsqrt(jnp.mean(x * x, axis=-1, keepdims=True) + EPS) * g
    return y * (1.0 + scale) + shift


def _norm_kernel(x_ref, g_ref, shift_ref, scale_ref, o_ref):
    o_ref[0] = _norm_mod(x_ref[0], g_ref[...], shift_ref[0], scale_ref[0]).astype(o_ref.dtype)


def _norm_call(x, g, shift, scale):
    b, l, d = x.shape
    tl = min(l, 512)
    vec = pl.BlockSpec((1, 1, d), lambda i, j: (i, 0, 0))
    return pl.pallas_call(
        _norm_kernel,
        grid=(b, l // tl),
        in_specs=[pl.BlockSpec((1, tl, d), lambda i, j: (i, j, 0)),
                  pl.BlockSpec((1, d), lambda i, j: (0, 0)), vec, vec],
        out_specs=pl.BlockSpec((1, tl, d), lambda i, j: (i, j, 0)),
        out_shape=jax.ShapeDtypeStruct((b, l, d), BF16),
        compiler_params=_params("arbitrary", "arbitrary"),
        name="norm",
    )(x, g.reshape(1, d), shift, scale)


def _proj_kernel(a_ref, w_ref, o_ref):
    o_ref[0] = jnp.dot(a_ref[...], w_ref[...].astype(BF16), preferred_element_type=F32)


def _proj_call(a, w, col_of, n_out, name):
    t, k = a.shape
    tn = 1024
    tm = min(t, 2048)
    return pl.pallas_call(
        _proj_kernel,
        grid=(t // tm, n_out),
        in_specs=[pl.BlockSpec((tm, k), lambda i, j: (i, 0)),
                  pl.BlockSpec((k, tn), lambda i, j: (0, col_of(j)))],
        out_specs=pl.BlockSpec((1, tm, tn), lambda i, j: (j, i, 0)),
        out_shape=jax.ShapeDtypeStruct((n_out, t, tn), F32),
        compiler_params=_params("arbitrary", "arbitrary"),
        name=name,
    )(a, w)


def _shift_rows(x, k):
    n = x.shape[0]
    row = lax.broadcasted_iota(jnp.int32, x.shape, 0)
    rolled = pltpu.roll(x, k % n, 0)
    keep = (row >= k) if k > 0 else (row < n + k)
    return jnp.where(keep, rolled, 0.0)


def _lru_kernel(ax_ref, axc_ref, cw_ref, cb_ref, wa_ref, wx_ref, ba_ref, bx_ref, lam_ref,
                o_ref, a_s, b_s):
    seq, width = ax_ref.shape[1], ax_ref.shape[2]
    seq_c = axc_ref.shape[1]
    heads = width // HEAD_DIM

    def conv(x):
        acc = _shift_rows(x, 2) * cw_ref[0:1, :]
        acc = acc + _shift_rows(x, 1) * cw_ref[1:2, :]
        acc = acc + x * cw_ref[2:3, :]
        acc = acc + _shift_rows(x, -1) * cw_ref[3:4, :]
        return acc + cb_ref[...]

    def coeffs(u, d, h):
        hs = slice(h * HEAD_DIM, (h + 1) * HEAD_DIM)
        ub = u.astype(BF16)
        r = jax.nn.sigmoid(_bdot(ub, wa_ref[d, h]) + ba_ref[d:d + 1, hs])
        i = jax.nn.sigmoid(_bdot(ub, wx_ref[d, h]) + bx_ref[d:d + 1, hs])
        log_a = -LRU_C * r * jax.nn.softplus(-lam_ref[d:d + 1, hs])
        a = jnp.exp(log_a)
        return a, jnp.sqrt(1.0 - a * a) * (i * u)

    u_c = conv(axc_ref[0])
    u_x = conv(ax_ref[0])
    n_blk = seq // SUBLANES
    row8 = lax.broadcasted_iota(jnp.int32, (SUBLANES, width), 0)

    for d, reverse in enumerate((False, True)):
        edge = u_c[seq_c - SUBLANES:] if reverse else u_c[:SUBLANES]
        pick = SUBLANES - 1 if reverse else 0
        h0 = jnp.concatenate([coeffs(edge[:, h * HEAD_DIM:(h + 1) * HEAD_DIM], d, h)[1][pick:pick + 1]
                              for h in range(heads)], axis=1)
        for h in range(heads):
            hs = slice(h * HEAD_DIM, (h + 1) * HEAD_DIM)
            a, b = coeffs(u_x[:, hs], d, h)
            a_s[:, hs] = a
            b_s[:, hs] = b

        def body(i, h_prev, reverse=reverse):
            blk = (n_blk - 1 - i) if reverse else i
            rows = pl.ds(pl.multiple_of(blk * SUBLANES, SUBLANES), SUBLANES)
            a = a_s[rows, :]
            b = b_s[rows, :]
            for s in (1, 2, 4):
                if reverse:
                    keep = row8 < SUBLANES - s
                    sh = SUBLANES - s
                else:
                    keep = row8 >= s
                    sh = s
                a_sh = jnp.where(keep, pltpu.roll(a, sh, 0), 1.0)
                b_sh = jnp.where(keep, pltpu.roll(b, sh, 0), 0.0)
                b = a * b_sh + b
                a = a * a_sh
            hh = b + a * h_prev
            if reverse:
                o_ref[0, rows, :] = o_ref[0, rows, :] + hh
                return hh[0:1, :]
            o_ref[0, rows, :] = hh
            return hh[SUBLANES - 1:SUBLANES, :]

        lax.fori_loop(0, n_blk, body, h0, unroll=4)


def _lru_call(z, zc, conv_w, conv_b, wa, wx, ba, bx, lam, bsz):
    t = z.shape[1]
    seq = t // bsz
    seq_c = zc.shape[1] // bsz
    width = 2 * HEAD_DIM
    n_w = (N_HEADS * HEAD_DIM) // width
    z4 = z.reshape(z.shape[0], bsz, seq, z.shape[2])
    zc4 = zc.reshape(zc.shape[0], bsz, seq_c, zc.shape[2])
    vec2 = pl.BlockSpec((2, width), lambda b, w: (0, w))
    wspec = pl.BlockSpec((2, width // HEAD_DIM, HEAD_DIM, HEAD_DIM), lambda b, w: (0, w, 0, 0))

    def kern(ax_ref, axc_ref, *rest):
        _lru_kernel(ax_ref.at[0], axc_ref.at[0], *rest)

    return pl.pallas_call(
        kern,
        grid=(bsz, n_w),
        in_specs=[pl.BlockSpec((1, 1, seq, width), lambda b, w: (0, b, 0, w)),
                  pl.BlockSpec((1, 1, seq_c, width), lambda b, w: (0, b, 0, w)),
                  pl.BlockSpec((CONV_W, width), lambda b, w: (0, w)),
                  pl.BlockSpec((1, width), lambda b, w: (0, w)),
                  wspec, wspec, vec2, vec2, vec2],
        out_specs=pl.BlockSpec((1, seq, width), lambda b, w: (b, 0, w)),
        out_shape=jax.ShapeDtypeStruct((bsz, seq, N_HEADS * HEAD_DIM), F32),
        scratch_shapes=[pltpu.VMEM((seq, width), F32), pltpu.VMEM((seq, width), F32)],
        compiler_params=_params("arbitrary", "arbitrary"),
        name="lru",
    )(z4, zc4, conv_w, conv_b.reshape(1, -1), wa, wx, ba, bx, lam)


def _split3(x):
    x1 = x.astype(BF16)
    r1 = x - x1.astype(F32)
    x2 = r1.astype(BF16)
    x3 = (r1 - x2.astype(F32)).astype(BF16)
    return x1, x2, x3


def _tri_cumsum(tri, x):
    x1, x2, x3 = _split3(x)
    dot = functools.partial(jnp.dot, preferred_element_type=F32)
    return dot(tri, x1) + dot(tri, x2) + dot(tri, x3)


def _hg_gates(f, lb):
    g = lb + (1.0 - lb) * jax.nn.sigmoid(f)
    return jnp.log(g), 1.0 - g


def _dot_tn(a, b):
    return lax.dot_general(a.astype(BF16), b.astype(BF16), (((0,), (0,)), ((), ())),
                           preferred_element_type=F32)


def _dot_nt(a, b):
    return lax.dot_general(a.astype(BF16), b.astype(BF16), (((1,), (1,)), ((), ())),
                           preferred_element_type=F32)


def _hgc_kernel(ff_ref, fb_ref, v_ref, lb_ref, sf_ref, sb_ref):
    n = ff_ref.shape[1]
    row = lax.broadcasted_iota(jnp.int32, (n, n), 0)
    col = lax.broadcasted_iota(jnp.int32, (n, n), 1)
    tri = (col <= row).astype(BF16)
    lb = lb_ref[...]
    v = v_ref[0]
    lg, k = _hg_gates(ff_ref[0], lb)
    lc = _tri_cumsum(tri, lg)
    kf = k * jnp.exp(lc[n - 1:n, :] - lc)
    lg, k = _hg_gates(fb_ref[0], lb)
    lc = _tri_cumsum(tri, lg)
    kb = k * jnp.exp(lc - lg)
    for h in range(N_HEADS):
        hs = slice(h * HEAD_DIM, (h + 1) * HEAD_DIM)
        sf_ref[0, h] = _dot_tn(v[:, hs], kf[:, hs])
        sb_ref[0, h] = _dot_tn(v[:, hs], kb[:, hs])


def _hgc_call(zc, lb, bsz):
    seq_c = zc.shape[1] // bsz
    width = zc.shape[2]
    zc4 = zc.reshape(zc.shape[0], bsz, seq_c, width)

    def slab(j):
        return pl.BlockSpec((1, 1, seq_c, width), lambda b: (j, b, 0, 0))

    def kern(ff_ref, fb_ref, v_ref, lb_ref, sf_ref, sb_ref):
        _hgc_kernel(ff_ref.at[0], fb_ref.at[0], v_ref.at[0], lb_ref, sf_ref, sb_ref)

    sspec = pl.BlockSpec((1, N_HEADS, HEAD_DIM, HEAD_DIM), lambda b: (b, 0, 0, 0))
    sshape = jax.ShapeDtypeStruct((bsz, N_HEADS, HEAD_DIM, HEAD_DIM), F32)
    return pl.pallas_call(
        kern,
        grid=(bsz,),
        in_specs=[slab(1), slab(2), slab(3), pl.BlockSpec((1, width), lambda b: (0, 0))],
        out_specs=[sspec, sspec],
        out_shape=[sshape, sshape],
        compiler_params=_params("arbitrary"),
        name="hgc",
    )(zc4, zc4, zc4, lb.reshape(1, width))


def _hg_kernel(q_ref, f_ref, v_ref, lb_ref, s0_ref, o_ref, st_s, *, reverse, n_chunks):
    width = N_HEADS * HEAD_DIM

    @pl.when(pl.program_id(1) == 0)
    def _():
        st_s[...] = s0_ref[0]

    row = lax.broadcasted_iota(jnp.int32, (CHUNK, CHUNK), 0)
    col = lax.broadcasted_iota(jnp.int32, (CHUNK, CHUNK), 1)
    causal = (col >= row) if reverse else (col <= row)
    tri = causal.astype(BF16)
    last, mid = (0, CHUNK - 1 - CHUNK // 2) if reverse else (CHUNK - 1, CHUNK // 2)
    lb = lb_ref[...]

    for cc in range(n_chunks):
        c = (n_chunks - 1 - cc) if reverse else cc
        cs = slice(c * width, (c + 1) * width)
        q = _silu(q_ref[0, :, cs])
        v = v_ref[0, :, cs]
        lg, k = _hg_gates(f_ref[0, :, cs], lb)
        lc = _tri_cumsum(tri, lg)
        ltot = lc[last:last + 1, :]
        lmid = lc[mid:mid + 1, :]
        e_tot = jnp.exp(ltot)
        q_in = q * jnp.exp(lc)
        q_at = q * jnp.exp(lc - lmid)
        k_at = k * jnp.exp(lmid - lc)
        k_st = k * jnp.exp(ltot - lc)
        for h in range(N_HEADS):
            hs = slice(h * HEAD_DIM, (h + 1) * HEAD_DIM)
            st = st_s[h]
            att = jnp.where(causal, _dot_nt(q_at[:, hs], k_at[:, hs]), 0.0)
            o = _dot_nt(q_in[:, hs], st) + _bdot(att, v[:, hs])
            o_ref[0, :, c * width + h * HEAD_DIM:c * width + (h + 1) * HEAD_DIM] = o
            st_s[h] = st * e_tot[:, hs] + _dot_tn(v[:, hs], k_st[:, hs])


def _hg_call(z, f_slab, lb, s0, bsz, reverse):
    t, width = z.shape[1], z.shape[2]
    seq = t // bsz
    rows = seq // GRID_W
    assert rows == CHUNK
    n_chunks = 8
    n_grp = GRID_W // n_chunks
    z4 = z.reshape(z.shape[0], bsz, rows, GRID_W * width)

    def grp(g):
        return (n_grp - 1 - g) if reverse else g

    def slab(j):
        return pl.BlockSpec((1, 1, rows, n_chunks * width), lambda b, g: (j, b, 0, grp(g)))

    def kern(q_ref, f_ref, v_ref, lb_ref, s0_ref, o_ref, st_s):
        _hg_kernel(q_ref.at[0], f_ref.at[0], v_ref.at[0], lb_ref, s0_ref, o_ref, st_s,
                   reverse=reverse, n_chunks=n_chunks)

    o = pl.pallas_call(
        kern,
        grid=(bsz, n_grp),
        in_specs=[slab(2), slab(f_slab), slab(5),
                  pl.BlockSpec((1, width), lambda b, g: (0, 0)),
                  pl.BlockSpec((1, N_HEADS, HEAD_DIM, HEAD_DIM), lambda b, g: (b, 0, 0, 0))],
        out_specs=pl.BlockSpec((1, rows, n_chunks * width), lambda b, g: (b, 0, grp(g))),
        out_shape=jax.ShapeDtypeStruct((bsz, rows, GRID_W * width), F32),
        scratch_shapes=[pltpu.VMEM((N_HEADS, HEAD_DIM, HEAD_DIM), F32)],
        compiler_params=_params("arbitrary", "arbitrary"),
        name="hg_bwd" if reverse else "hg_fwd",
    )(z4, z4, z4, lb.reshape(1, width), s0)
    return o.reshape(bsz, seq, width)


def _merge_kernel(x_ref, ya_ref, ay_ref, of_ref, ob_ref, og_ref, ga_ref, gb_ref,
                  gate_ref, shift_ref, scale_ref, hgn_ref, n2_ref,
                  wpa_ref, wpb_ref, wo_ref, wr_ref, br_ref,
                  x1_ref, h2_ref, route_ref, cnt_ref, run_s):
    tm = x_ref.shape[1]

    @pl.when((pl.program_id(0) == 0) & (pl.program_id(1) == 0))
    def _():
        run_s[...] = jnp.zeros_like(run_s)

    pa = _bdot(ya_ref[0] * jax.nn.gelu(ay_ref[0, 0]), wpa_ref[...])
    yb = of_ref[0] + ob_ref[0]
    parts = []
    for h in range(N_HEADS):
        hs = slice(h * HEAD_DIM, (h + 1) * HEAD_DIM)
        o = yb[:, hs]
        parts.append(o * lax.rsqrt(jnp.mean(o * o, axis=-1, keepdims=True) + EPS) * hgn_ref[:, hs])
    yb = jnp.concatenate(parts, axis=1)
    pb = _bdot(yb * _silu(og_ref[0, 0]), wpb_ref[...])
    merged = jax.nn.sigmoid(ga_ref[0, 0]) * pa + jax.nn.sigmoid(gb_ref[0, 0]) * pb
    x1 = x_ref[0] + gate_ref[0] * _bdot(merged, wo_ref[...])
    x1_ref[0] = x1
    h2 = _norm_mod(x1, n2_ref[...], shift_ref[0], scale_ref[0])
    h2_ref[0] = h2

    logit = jnp.dot(h2, wr_ref[...], precision=HIGHEST, preferred_element_type=F32) + br_ref[...]
    col = lax.broadcasted_iota(jnp.int32, (tm, LANES), 1)
    colf = col.astype(F32)
    neg = -jnp.inf

    def first_at(mask):
        return -jnp.max(jnp.where(mask, -colf, -1e9), axis=1, keepdims=True)

    is_g = (col >= N_EXPERTS) & (col < N_EXPERTS + N_GROUPS)
    lgm = jnp.where(is_g, logit, neg)
    mg = jnp.max(lgm, axis=1, keepdims=True)
    pg = 1.0 / jnp.sum(jnp.exp(lgm - mg), axis=1, keepdims=True)
    gi = first_at(is_g & (logit == mg)) - float(N_EXPERTS)
    is_e = (col < N_EXPERTS) & ((col // EXP_PER_GROUP).astype(F32) == gi)
    lem = jnp.where(is_e, logit, neg)
    ee = jnp.exp(lem - jnp.max(lem, axis=1, keepdims=True))
    pe = ee / jnp.sum(ee, axis=1, keepdims=True)
    p1 = jnp.max(jnp.where(is_e, pe, -1.0), axis=1, keepdims=True)
    i1 = first_at(is_e & (pe == p1))
    rest = is_e & (colf != i1)
    p2 = jnp.max(jnp.where(rest, pe, -1.0), axis=1, keepdims=True)
    i2 = first_at(rest & (pe == p2))
    den = p1 + p2

    r_i = lax.broadcasted_iota(jnp.int32, (tm, tm), 0)
    c_i = lax.broadcasted_iota(jnp.int32, (tm, tm), 1)
    earlier = (c_i < r_i).astype(BF16)
    oh1 = (colf == i1).astype(F32)
    oh2 = (colf == i2).astype(F32)
    run = run_s[...]
    rank1 = jnp.sum(oh1 * (_bdot(earlier, oh1) + run), axis=1, keepdims=True)
    run = run + jnp.sum(oh1, axis=0, keepdims=True)
    rank2 = jnp.sum(oh2 * (_bdot(earlier, oh2) + run), axis=1, keepdims=True)
    run = run + jnp.sum(oh2, axis=0, keepdims=True)
    run_s[...] = run
    cnt_ref[...] = run
    vals = (i1, i2, pg * p1 / den, pg * p2 / den, rank1, rank2)
    route_ref[0] = sum(jnp.where(col == k, v, 0.0) for k, v in enumerate(vals))


def _merge_call(x, ya, z, o_f, o_b, gate, shift, scale, hg_norm_g, norm2_g,
                w_proj_a, w_proj_b, w_out, w_router, b_router):
    bsz, seq, d = x.shape
    tm = min(seq, 256)
    z4 = z.reshape(z.shape[0], bsz, seq, d)
    tok = pl.BlockSpec((1, tm, d), lambda b, i: (b, i, 0))

    def slab(j):
        return pl.BlockSpec((1, 1, tm, d), lambda b, i: (j, b, i, 0))

    vec = pl.BlockSpec((1, 1, d), lambda b, i: (b, 0, 0))
    row = pl.BlockSpec((1, d), lambda b, i: (0, 0))
    wsp = pl.BlockSpec((d, d), lambda b, i: (0, 0))
    return pl.pallas_call(
        _merge_kernel,
        grid=(bsz, seq // tm),
        in_specs=[tok, tok, slab(1), tok, tok, slab(6), slab(7), slab(8),
                  vec, vec, vec, row, row, wsp, wsp, wsp,
                  pl.BlockSpec((d, LANES), lambda b, i: (0, 0)),
                  pl.BlockSpec((1, LANES), lambda b, i: (0, 0))],
        out_specs=[tok, tok, pl.BlockSpec((1, tm, LANES), lambda b, i: (b, i, 0)),
                   pl.BlockSpec((1, LANES), lambda b, i: (0, 0))],
        out_shape=[jax.ShapeDtypeStruct((bsz, seq, d), F32),
                   jax.ShapeDtypeStruct((bsz, seq, d), F32),
                   jax.ShapeDtypeStruct((bsz, seq, LANES), F32),
                   jax.ShapeDtypeStruct((1, LANES), F32)],
        scratch_shapes=[pltpu.VMEM((1, LANES), F32)],
        compiler_params=_params("arbitrary", "arbitrary"),
        name="merge",
    )(x, ya, z4, o_f, o_b, z4, z4, z4, gate, shift, scale,
      hg_norm_g.reshape(1, d), norm2_g.reshape(1, d),
      w_proj_a.astype(BF16), w_proj_b.astype(BF16), w_out.astype(BF16), w_router, b_router)


MOE_TILE = 256


def _row_copy(src_ref, src_row, dst_ref, dst_row, sem):
    return pltpu.make_async_copy(src_ref.at[pl.ds(src_row, 1), :], dst_ref.at[pl.ds(dst_row, 1), :], sem)


def _dispatch_kernel(pos_ref, h_ref, xs_in_ref, xs_ref, sem):
    del xs_in_ref
    tm = h_ref.shape[0]

    def copies(r):
        return [_row_copy(h_ref, r, xs_ref, pos_ref[0, 0, k * tm + r], sem) for k in range(2)]

    def start(r, carry):
        for cp in copies(r):
            cp.start()
        return carry

    def wait(r, carry):
        for cp in copies(r):
            cp.wait()
        return carry

    lax.fori_loop(0, tm, start, 0)
    lax.fori_loop(0, tm, wait, 0)


def _dispatch_call(pos, h2, n_slots):
    t, d = h2.shape
    tm = pos.shape[2] // 2
    return pl.pallas_call(
        _dispatch_kernel,
        grid=(t // tm,),
        in_specs=[pl.BlockSpec((1, 1, 2 * tm), lambda i: (i, 0, 0), memory_space=pltpu.SMEM),
                  pl.BlockSpec((tm, d), lambda i: (i, 0)),
                  pl.BlockSpec(memory_space=pltpu.HBM)],
        out_specs=pl.BlockSpec(memory_space=pltpu.HBM),
        out_shape=jax.ShapeDtypeStruct((n_slots, d), F32),
        scratch_shapes=[pltpu.SemaphoreType.DMA(())],
        input_output_aliases={2: 0},
        compiler_params=_params("arbitrary"),
        name="dispatch",
    )(pos, h2, jnp.zeros((n_slots, d), F32))


def _experts_kernel(te_ref, nu_ref, xs_ref, w1_ref, w3_ref, w2_ref, ys_ref, w1_s, w3_s, w2_s):
    i = pl.program_id(0)

    @pl.when(i < nu_ref[0])
    def _():
        @pl.when((i == 0) | (te_ref[i] != te_ref[jnp.maximum(i - 1, 0)]))
        def _():
            w1_s[...] = w1_ref[0].astype(BF16)
            w3_s[...] = w3_ref[0].astype(BF16)
            w2_s[...] = w2_ref[0].astype(BF16)

        xb = xs_ref[...].astype(BF16)
        up = _silu(_bdot(xb, w1_s[...])) * _bdot(xb, w3_s[...])
        ys_ref[...] = _bdot(up, w2_s[...])

    @pl.when(i >= nu_ref[0])
    def _():
        ys_ref[...] = jnp.zeros_like(ys_ref)


def _experts_call(tile_expert, n_used, xs, w1, w3, w2):
    n_slots, d = xs.shape
    _, _, d_e = w1.shape
    n_tiles = n_slots // MOE_TILE
    tok = pl.BlockSpec((MOE_TILE, d), lambda i, te, nu: (jnp.minimum(i, nu[0] - 1), 0))
    return pl.pallas_call(
        _experts_kernel,
        grid_spec=pltpu.PrefetchScalarGridSpec(
            num_scalar_prefetch=2,
            grid=(n_tiles,),
            in_specs=[tok,
                      pl.BlockSpec((1, d, d_e), lambda i, te, nu: (te[i], 0, 0)),
                      pl.BlockSpec((1, d, d_e), lambda i, te, nu: (te[i], 0, 0)),
                      pl.BlockSpec((1, d_e, d), lambda i, te, nu: (te[i], 0, 0))],
            out_specs=pl.BlockSpec((MOE_TILE, d), lambda i, te, nu: (i, 0)),
            scratch_shapes=[pltpu.VMEM((d, d_e), BF16), pltpu.VMEM((d, d_e), BF16), pltpu.VMEM((d_e, d), BF16)]),
        out_shape=jax.ShapeDtypeStruct((n_slots, d), F32),
        compiler_params=_params("arbitrary"),
        name="experts",
    )(tile_expert, n_used, xs, w1, w3, w2)


def _combine_kernel(pos_ref, route_ref, x1_ref, gate_ref, fg_ref, ys_ref, o_ref, y1_s, y2_s, sem):
    tm = x1_ref.shape[0]

    def copies(r):
        return [_row_copy(ys_ref, pos_ref[0, 0, r], y1_s, r, sem),
                _row_copy(ys_ref, pos_ref[0, 0, tm + r], y2_s, r, sem)]

    def start(r, carry):
        for cp in copies(r):
            cp.start()
        return carry

    def wait(r, carry):
        for cp in copies(r):
            cp.wait()
        return carry

    lax.fori_loop(0, tm, start, 0)
    lax.fori_loop(0, tm, wait, 0)
    route = route_ref[...]
    y = route[:, 2:3] * y1_s[...] + route[:, 3:4] * y2_s[...]
    x2 = x1_ref[...] + gate_ref[0] * y
    o_ref[...] = x2 * lax.rsqrt(jnp.mean(x2 * x2, axis=-1, keepdims=True) + EPS) * fg_ref[...]


def _combine_call(pos, route, x1, gate, final_g, ys, seq):
    t, d = x1.shape
    tm = pos.shape[2] // 2
    per_b = seq // tm
    tok = pl.BlockSpec((tm, d), lambda i: (i, 0))
    return pl.pallas_call(
        _combine_kernel,
        grid=(t // tm,),
        in_specs=[pl.BlockSpec((1, 1, 2 * tm), lambda i: (i, 0, 0), memory_space=pltpu.SMEM),
                  pl.BlockSpec((tm, LANES), lambda i: (i, 0)), tok,
                  pl.BlockSpec((1, 1, d), lambda i: (i // per_b, 0, 0)),
                  pl.BlockSpec((1, d), lambda i: (0, 0)),
                  pl.BlockSpec(memory_space=pltpu.HBM)],
        out_specs=tok,
        out_shape=jax.ShapeDtypeStruct((t, d), F32),
        scratch_shapes=[pltpu.VMEM((tm, d), F32), pltpu.VMEM((tm, d), F32), pltpu.SemaphoreType.DMA(())],
        compiler_params=_params("arbitrary"),
        name="combine",
    )(pos, route, x1, gate, final_g.reshape(1, d), ys)


def _moe_plan(route, cnt, tm):
    t = route.shape[0]
    counts = cnt[0, :N_EXPERTS].astype(jnp.int32)
    n_tiles_e = (counts + MOE_TILE - 1) // MOE_TILE
    tile_end = jnp.cumsum(n_tiles_e)
    offsets = (tile_end - n_tiles_e) * MOE_TILE
    n_used = tile_end[-1]
    experts = jnp.arange(N_EXPERTS, dtype=jnp.int32)

    def slot(eid, rank):
        base = jnp.sum(jnp.where(eid.astype(jnp.int32)[:, None] == experts[None, :], offsets[None, :], 0), axis=1)
        return base + rank.astype(jnp.int32)

    pos1 = slot(route[:, 0], route[:, 4]).reshape(t // tm, tm)
    pos2 = slot(route[:, 1], route[:, 5]).reshape(t // tm, tm)
    pos = jnp.concatenate([pos1, pos2], axis=1)[:, None, :]
    n_tiles = (2 * t) // MOE_TILE + N_EXPERTS
    tiles = jnp.arange(n_tiles, dtype=jnp.int32)
    tile_expert = jnp.sum(jnp.minimum(tiles, n_used - 1)[:, None] >= tile_end[None, :], axis=1).astype(jnp.int32)
    return pos, tile_expert, n_used.reshape(1).astype(jnp.int32), n_tiles * MOE_TILE


def kernel(x, c, ctx, c_ctx, w_mod, b_mod, norm1_g, w_in, conv_w, conv_b, lru_wa, lru_ba, lru_wx, lru_bx,
           lru_lam, hg_lb, hg_norm_g, w_proj_a, w_proj_b, w_out, norm2_g, w_rg, b_rg, w_re, b_re,
           w1, w3, w2, final_g):
    bsz, seq, d = x.shape
    seq_c = ctx.shape[1]
    l = 0

    pad = (-(bsz + 1)) % SUBLANES
    cc = jnp.concatenate([c, c_ctx[None, :], jnp.zeros((pad, d), F32)], axis=0)
    mod = _mod_call(cc, w_mod[l], b_mod[l])
    mx = [mod[:bsz, None, k * d:(k + 1) * d] for k in range(6)]
    mc = [jnp.broadcast_to(mod[bsz:bsz + 1, None, k * d:(k + 1) * d], (bsz, 1, d)) for k in range(2)]
    lb = jax.nn.softmax(hg_lb.astype(F32), axis=0)[l]

    hx = _norm_call(x, norm1_g[l], mx[0], mx[1]).reshape(bsz * seq, d)
    hc = _norm_call(ctx, norm1_g[l], mc[0], mc[1]).reshape(bsz * seq_c, d)
    z = _proj_call(hx, w_in[l], lambda j: j, 9, "proj_x")
    zc = _proj_call(hc, w_in[l], lambda j: jnp.where(j > 0, j + 2, 0), 4, "proj_c")

    ya = _lru_call(z, zc, conv_w[l], conv_b[l], lru_wa[l], lru_wx[l], lru_ba[l], lru_bx[l], lru_lam[l], bsz)
    s_f, s_b = _hgc_call(zc, lb, bsz)
    o_f = _hg_call(z, 3, lb, s_f, bsz, reverse=False)
    o_b = _hg_call(z, 4, lb, s_b, bsz, reverse=True)

    w_router = jnp.zeros((d, LANES), F32).at[:, :N_EXPERTS].set(w_re[l])
    w_router = w_router.at[:, N_EXPERTS:N_EXPERTS + N_GROUPS].set(w_rg[l])
    b_router = jnp.zeros((1, LANES), F32).at[0, :N_EXPERTS].set(b_re[l])
    b_router = b_router.at[0, N_EXPERTS:N_EXPERTS + N_GROUPS].set(b_rg[l])
    x1, h2, route, cnt = _merge_call(x, ya, z, o_f, o_b, mx[2], mx[3], mx[4], hg_norm_g[l], norm2_g[l],
                                     w_proj_a[l], w_proj_b[l], w_out[l], w_router, b_router)

    route = route.reshape(bsz * seq, LANES)
    pos, tile_expert, n_used, n_slots = _moe_plan(route, cnt, min(seq, 512))
    xs = _dispatch_call(pos, h2.reshape(bsz * seq, d), n_slots)
    ys = _experts_call(tile_expert, n_used, xs, w1[l], w3[l], w2[l])
    out = _combine_call(pos, route, x1.reshape(bsz * seq, d), mx[5], final_g, ys, seq)
    return out.reshape(bsz, seq, d)
```

```python
import functools

import jax
import jax.numpy as jnp
from jax import lax
from jax.experimental import pallas as pl
from jax.experimental.pallas import tpu as pltpu

F32 = jnp.float32
BF16 = jnp.bfloat16
HIGHEST = lax.Precision.HIGHEST

EPS = 1e-6
GRID_W = 64
CHUNK = 32
LRU_C = 8.0
CONV_W = 4
N_HEADS = 8
HEAD_DIM = 128
N_GROUPS = 4
EXP_PER_GROUP = 8
N_EXPERTS = N_GROUPS * EXP_PER_GROUP
LANES = 128
SUBLANES = 8
VMEM_LIMIT = 56 * 1024 * 1024


def _params(*sem):
    return pltpu.CompilerParams(dimension_semantics=sem, vmem_limit_bytes=VMEM_LIMIT)


def _silu(x):
    return x * jax.nn.sigmoid(x)


def _bdot(a, b):
    return jnp.dot(a.astype(BF16), b.astype(BF16), preferred_element_type=F32)


def _mod_kernel(c_ref, w_ref, b_ref, o_ref):
    o_ref[...] = jnp.dot(_silu(c_ref[...]), w_ref[...], precision=HIGHEST,
                         preferred_element_type=F32) + b_ref[...]


def _mod_call(cc, w_mod, b_mod):
    rows, d = cc.shape
    n = w_mod.shape[1]
    tn = 1024
    return pl.pallas_call(
        _mod_kernel,
        grid=(n // tn,),
        in_specs=[pl.BlockSpec((rows, d), lambda j: (0, 0)),
                  pl.BlockSpec((d, tn), lambda j: (0, j)),
                  pl.BlockSpec((1, tn), lambda j: (0, j))],
        out_specs=pl.BlockSpec((rows, tn), lambda j: (0, j)),
        out_shape=jax.ShapeDtypeStruct((rows, n), F32),
        compiler_params=_params("arbitrary"),
        name="mod",
    )(cc, w_mod, b_mod.reshape(1, n))


def _norm_mod(x, g, shift, scale):
    y = x * lax.rsqrt(jnp.mean(x * x, axis=-1, keepdims=True) + EPS) * g
    return y * (1.0 + scale) + shift


def _norm_kernel(x_ref, g_ref, shift_ref, scale_ref, o_ref):
    o_ref[0] = _norm_mod(x_ref[0], g_ref[...], shift_ref[0], scale_ref[0]).astype(o_ref.dtype)


def _norm_call(x, g, shift, scale):
    b, l, d = x.shape
    tl = min(l, 512)
    vec = pl.BlockSpec((1, 1, d), lambda i, j: (i, 0, 0))
    return pl.pallas_call(
        _norm_kernel,
        grid=(b, l // tl),
        in_specs=[pl.BlockSpec((1, tl, d), lambda i, j: (i, j, 0)),
                  pl.BlockSpec((1, d), lambda i, j: (0, 0)), vec, vec],
        out_specs=pl.BlockSpec((1, tl, d), lambda i, j: (i, j, 0)),
        out_shape=jax.ShapeDtypeStruct((b, l, d), BF16),
        compiler_params=_params("arbitrary", "arbitrary"),
        name="norm",
    )(x, g.reshape(1, d), shift, scale)


def _proj_kernel(a_ref, w_ref, o_ref):
    o_ref[0] = jnp.dot(a_ref[...], w_ref[...].astype(BF16), preferred_element_type=F32)


def _proj_call(a, w, col_of, n_out, name):
    t, k = a.shape
    tn = 1024
    tm = min(t, 2048)
    return pl.pallas_call(
        _proj_kernel,
        grid=(t // tm, n_out),
        in_specs=[pl.BlockSpec((tm, k), lambda i, j: (i, 0)),
                  pl.BlockSpec((k, tn), lambda i, j: (0, col_of(j)))],
        out_specs=pl.BlockSpec((1, tm, tn), lambda i, j: (j, i, 0)),
        out_shape=jax.ShapeDtypeStruct((n_out, t, tn), F32),
        compiler_params=_params("arbitrary", "arbitrary"),
        name=name,
    )(a, w)


def _shift_rows(x, k):
    n = x.shape[0]
    row = lax.broadcasted_iota(jnp.int32, x.shape, 0)
    rolled = pltpu.roll(x, k % n, 0)
    keep = (row >= k) if k > 0 else (row < n + k)
    return jnp.where(keep, rolled, 0.0)


def _lru_kernel(ax_ref, axc_ref, cw_ref, cb_ref, wa_ref, wx_ref, ba_ref, bx_ref, lam_ref,
                o_ref, a_s, b_s):
    seq, width = ax_ref.shape[1], ax_ref.shape[2]
    seq_c = axc_ref.shape[1]
    heads = width // HEAD_DIM

    def conv(x):
        acc = _shift_rows(x, 2) * cw_ref[0:1, :]
        acc = acc + _shift_rows(x, 1) * cw_ref[1:2, :]
        acc = acc + x * cw_ref[2:3, :]
        acc = acc + _shift_rows(x, -1) * cw_ref[3:4, :]
        return acc + cb_ref[...]

    def coeffs(u, d, h):
        hs = slice(h * HEAD_DIM, (h + 1) * HEAD_DIM)
        ub = u.astype(BF16)
        r = jax.nn.sigmoid(_bdot(ub, wa_ref[d, h]) + ba_ref[d:d + 1, hs])
        i = jax.nn.sigmoid(_bdot(ub, wx_ref[d, h]) + bx_ref[d:d + 1, hs])
        log_a = -LRU_C * r * jax.nn.softplus(-lam_ref[d:d + 1, hs])
        a = jnp.exp(log_a)
        return a, jnp.sqrt(1.0 - a * a) * (i * u)

    u_c = conv(axc_ref[0])
    u_x = conv(ax_ref[0])
    n_blk = seq // SUBLANES
    row8 = lax.broadcasted_iota(jnp.int32, (SUBLANES, width), 0)

    for d, reverse in enumerate((False, True)):
        edge = u_c[seq_c - SUBLANES:] if reverse else u_c[:SUBLANES]
        pick = SUBLANES - 1 if reverse else 0
        h0 = jnp.concatenate([coeffs(edge[:, h * HEAD_DIM:(h + 1) * HEAD_DIM], d, h)[1][pick:pick + 1]
                              for h in range(heads)], axis=1)
        for h in range(heads):
            hs = slice(h * HEAD_DIM, (h + 1) * HEAD_DIM)
            a, b = coeffs(u_x[:, hs], d, h)
            a_s[:, hs] = a
            b_s[:, hs] = b

        def body(i, h_prev, reverse=reverse):
            blk = (n_blk - 1 - i) if reverse else i
            rows = pl.ds(pl.multiple_of(blk * SUBLANES, SUBLANES), SUBLANES)
            a = a_s[rows, :]
            b = b_s[rows, :]
            for s in (1, 2, 4):
                if reverse:
                    keep = row8 < SUBLANES - s
                    sh = SUBLANES - s
                else:
                    keep = row8 >= s
                    sh = s
                a_sh = jnp.where(keep, pltpu.roll(a, sh, 0), 1.0)
                b_sh = jnp.where(keep, pltpu.roll(b, sh, 0), 0.0)
                b = a * b_sh + b
                a = a * a_sh
            hh = b + a * h_prev
            if reverse:
                o_ref[0, rows, :] = o_ref[0, rows, :] + hh
                return hh[0:1, :]
            o_ref[0, rows, :] = hh
            return hh[SUBLANES - 1:SUBLANES, :]

        lax.fori_loop(0, n_blk, body, h0, unroll=4)


def _lru_call(z, zc, conv_w, conv_b, wa, wx, ba, bx, lam, bsz):
    t = z.shape[1]
    seq = t // bsz
    seq_c = zc.shape[1] // bsz
    width = 2 * HEAD_DIM
    n_w = (N_HEADS * HEAD_DIM) // width
    z4 = z.reshape(z.shape[0], bsz, seq, z.shape[2])
    zc4 = zc.reshape(zc.shape[0], bsz, seq_c, zc.shape[2])
    vec2 = pl.BlockSpec((2, width), lambda b, w: (0, w))
    wspec = pl.BlockSpec((2, width // HEAD_DIM, HEAD_DIM, HEAD_DIM), lambda b, w: (0, w, 0, 0))

    def kern(ax_ref, axc_ref, *rest):
        _lru_kernel(ax_ref.at[0], axc_ref.at[0], *rest)

    return pl.pallas_call(
        kern,
        grid=(bsz, n_w),
        in_specs=[pl.BlockSpec((1, 1, seq, width), lambda b, w: (0, b, 0, w)),
                  pl.BlockSpec((1, 1, seq_c, width), lambda b, w: (0, b, 0, w)),
                  pl.BlockSpec((CONV_W, width), lambda b, w: (0, w)),
                  pl.BlockSpec((1, width), lambda b, w: (0, w)),
                  wspec, wspec, vec2, vec2, vec2],
        out_specs=pl.BlockSpec((1, seq, width), lambda b, w: (b, 0, w)),
        out_shape=jax.ShapeDtypeStruct((bsz, seq, N_HEADS * HEAD_DIM), F32),
        scratch_shapes=[pltpu.VMEM((seq, width), F32), pltpu.VMEM((seq, width), F32)],
        compiler_params=_params("arbitrary", "arbitrary"),
        name="lru",
    )(z4, zc4, conv_w, conv_b.reshape(1, -1), wa, wx, ba, bx, lam)


def _split3(x):
    x1 = x.astype(BF16)
    r1 = x - x1.astype(F32)
    x2 = r1.astype(BF16)
    x3 = (r1 - x2.astype(F32)).astype(BF16)
    return x1, x2, x3


def _tri_cumsum(tri, x):
    x1, x2, x3 = _split3(x)
    dot = functools.partial(jnp.dot, preferred_element_type=F32)
    return dot(tri, x1) + dot(tri, x2) + dot(tri, x3)


def _hg_gates(f, lb):
    g = lb + (1.0 - lb) * jax.nn.sigmoid(f)
    return jnp.log(g), 1.0 - g


def _dot_tn(a, b):
    return lax.dot_general(a.astype(BF16), b.astype(BF16), (((0,), (0,)), ((), ())),
                           preferred_element_type=F32)


def _dot_nt(a, b):
    return lax.dot_general(a.astype(BF16), b.astype(BF16), (((1,), (1,)), ((), ())),
                           preferred_element_type=F32)


def _hgc_kernel(ff_ref, fb_ref, v_ref, lb_ref, sf_ref, sb_ref):
    n = ff_ref.shape[1]
    row = lax.broadcasted_iota(jnp.int32, (n, n), 0)
    col = lax.broadcasted_iota(jnp.int32, (n, n), 1)
    tri = (col <= row).astype(BF16)
    lb = lb_ref[...]
    v = v_ref[0]
    lg, k = _hg_gates(ff_ref[0], lb)
    lc = _tri_cumsum(tri, lg)
    kf = k * jnp.exp(lc[n - 1:n, :] - lc)
    lg, k = _hg_gates(fb_ref[0], lb)
    lc = _tri_cumsum(tri, lg)
    kb = k * jnp.exp(lc - lg)
    for h in range(N_HEADS):
        hs = slice(h * HEAD_DIM, (h + 1) * HEAD_DIM)
        sf_ref[0, h] = _dot_tn(v[:, hs], kf[:, hs])
        sb_ref[0, h] = _dot_tn(v[:, hs], kb[:, hs])


def _hgc_call(zc, lb, bsz):
    seq_c = zc.shape[1] // bsz
    width = zc.shape[2]
    zc4 = zc.reshape(zc.shape[0], bsz, seq_c, width)

    def slab(j):
        return pl.BlockSpec((1, 1, seq_c, width), lambda b: (j, b, 0, 0))

    def kern(ff_ref, fb_ref, v_ref, lb_ref, sf_ref, sb_ref):
        _hgc_kernel(ff_ref.at[0], fb_ref.at[0], v_ref.at[0], lb_ref, sf_ref, sb_ref)

    sspec = pl.BlockSpec((1, N_HEADS, HEAD_DIM, HEAD_DIM), lambda b: (b, 0, 0, 0))
    sshape = jax.ShapeDtypeStruct((bsz, N_HEADS, HEAD_DIM, HEAD_DIM), F32)
    return pl.pallas_call(
        kern,
        grid=(bsz,),
        in_specs=[slab(1), slab(2), slab(3), pl.BlockSpec((1, width), lambda b: (0, 0))],
        out_specs=[sspec, sspec],
        out_shape=[sshape, sshape],
        compiler_params=_params("arbitrary"),
        name="hgc",
    )(zc4, zc4, zc4, lb.reshape(1, width))


def _hg_kernel(q_ref, f_ref, v_ref, lb_ref, s0_ref, o_ref, st_s, *, reverse, n_chunks, post):
    width = N_HEADS * HEAD_DIM
    rows = n_chunks * CHUNK

    @pl.when(pl.program_id(1) == 0)
    def _():
        st_s[...] = s0_ref[0]

    row = lax.broadcasted_iota(jnp.int32, (rows, rows), 0)
    col = lax.broadcasted_iota(jnp.int32, (rows, rows), 1)
    same = (row // CHUNK) == (col // CHUNK)
    causal = same & ((col >= row) if reverse else (col <= row))
    tri = causal.astype(BF16)
    last, mid = (0, CHUNK - 1 - CHUNK // 2) if reverse else (CHUNK - 1, CHUNK // 2)

    def chunked(t):
        return t.reshape(n_chunks, CHUNK, width)

    def flat(t):
        return t.reshape(rows, width)

    q = chunked(_silu(q_ref[0]))
    v = v_ref[0]
    lg, k = _hg_gates(f_ref[0], lb_ref[...])
    k = chunked(k)
    lc = chunked(_tri_cumsum(tri, lg))
    ltot = lc[:, last:last + 1, :]
    lmid = lc[:, mid:mid + 1, :]
    e_tot = jnp.exp(ltot)
    q_in = flat(q * jnp.exp(lc))
    q_at = flat(q * jnp.exp(lc - lmid))
    k_at = flat(k * jnp.exp(lmid - lc))
    k_st = flat(k * jnp.exp(ltot - lc))

    outs = []
    for h in range(N_HEADS):
        hs = slice(h * HEAD_DIM, (h + 1) * HEAD_DIM)
        att = jnp.where(causal, _dot_nt(q_at[:, hs], k_at[:, hs]), 0.0)
        o_intra = _bdot(att, v[:, hs])
        st = st_s[h]
        o_inter = [None] * n_chunks
        for cc in range(n_chunks):
            c = (n_chunks - 1 - cc) if reverse else cc
            rs = slice(c * CHUNK, (c + 1) * CHUNK)
            o_inter[c] = _dot_nt(q_in[rs, hs], st)
            st = st * e_tot[c, :, hs] + _dot_tn(v[rs, hs], k_st[rs, hs])
        st_s[h] = st
        outs.append(o_intra + jnp.concatenate(o_inter, axis=0))

    if post is None:
        for h in range(N_HEADS):
            o_ref[0, :, h * HEAD_DIM:(h + 1) * HEAD_DIM] = outs[h]
        return
    of_ref, og_ref, gb_ref, hgn_ref, wpb_ref = post
    parts = []
    for h in range(N_HEADS):
        hs = slice(h * HEAD_DIM, (h + 1) * HEAD_DIM)
        o = of_ref[0, :, hs] + outs[h]
        parts.append(o * lax.rsqrt(jnp.mean(o * o, axis=-1, keepdims=True) + EPS) * hgn_ref[:, hs])
    yb = jnp.concatenate(parts, axis=1)
    o_ref[0] = jax.nn.sigmoid(gb_ref[0]) * _bdot(yb * _silu(og_ref[0]), wpb_ref[...])


def _hg_call(zh, f_slab, lb, s0, bsz, reverse, post=None):
    t, width = zh.shape[1], zh.shape[2]
    seq = t // bsz
    assert seq // GRID_W == CHUNK
    n_chunks = 8
    rows = n_chunks * CHUNK
    n_grp = seq // rows
    z4 = zh.reshape(zh.shape[0], bsz, seq, width)

    def grp(g):
        return (n_grp - 1 - g) if reverse else g

    def slab(j):
        return pl.BlockSpec((1, 1, rows, width), lambda b, g: (j, b, grp(g), 0))

    tok = pl.BlockSpec((1, rows, width), lambda b, g: (b, grp(g), 0))
    row = pl.BlockSpec((1, width), lambda b, g: (0, 0))
    in_specs = [slab(0), slab(f_slab), slab(3), row,
                pl.BlockSpec((1, N_HEADS, HEAD_DIM, HEAD_DIM), lambda b, g: (b, 0, 0, 0))]
    args = [z4, z4, z4, lb.reshape(1, width), s0]
    if post is not None:
        o_other, hg_norm_g, w_proj_b = post
        in_specs += [tok, slab(4), slab(5), row, pl.BlockSpec((width, width), lambda b, g: (0, 0))]
        args += [o_other, z4, z4, hg_norm_g.reshape(1, width), w_proj_b.astype(BF16)]

    def kern(q_ref, f_ref, v_ref, lb_ref, s0_ref, *rest):
        *extra, o_ref, st_s = rest
        if extra:
            of_ref, og_ref, gb_ref, hgn_ref, wpb_ref = extra
            extra = (of_ref, og_ref.at[0], gb_ref.at[0], hgn_ref, wpb_ref)
        _hg_kernel(q_ref.at[0], f_ref.at[0], v_ref.at[0], lb_ref, s0_ref, o_ref, st_s,
                   reverse=reverse, n_chunks=n_chunks, post=extra or None)

    return pl.pallas_call(
        kern,
        grid=(bsz, n_grp),
        in_specs=in_specs,
        out_specs=tok,
        out_shape=jax.ShapeDtypeStruct((bsz, seq, width), F32),
        scratch_shapes=[pltpu.VMEM((N_HEADS, HEAD_DIM, HEAD_DIM), F32)],
        compiler_params=_params("arbitrary", "arbitrary"),
        name="hg_bwd" if reverse else "hg_fwd",
    )(*args)


def _merge_kernel(x_ref, ya_ref, ay_ref, ga_ref, mb_ref,
                  gate_ref, shift_ref, scale_ref, n2_ref,
                  wpa_ref, wo_ref, wr_ref, br_ref,
                  x1_ref, h2_ref, route_ref, cnt_ref, run_s):
    tm = x_ref.shape[1]

    @pl.when((pl.program_id(0) == 0) & (pl.program_id(1) == 0))
    def _():
        run_s[...] = jnp.zeros_like(run_s)

    pa = _bdot(ya_ref[0] * jax.nn.gelu(ay_ref[0, 0]), wpa_ref[...])
    merged = jax.nn.sigmoid(ga_ref[0, 0]) * pa + mb_ref[0]
    x1 = x_ref[0] + gate_ref[0] * _bdot(merged, wo_ref[...])
    x1_ref[0] = x1
    h2 = _norm_mod(x1, n2_ref[...], shift_ref[0], scale_ref[0])
    h2_ref[0] = h2

    logit = jnp.dot(h2, wr_ref[...], precision=HIGHEST, preferred_element_type=F32) + br_ref[...]
    col = lax.broadcasted_iota(jnp.int32, (tm, LANES), 1)
    colf = col.astype(F32)
    neg = -jnp.inf

    def first_at(mask):
        return -jnp.max(jnp.where(mask, -colf, -1e9), axis=1, keepdims=True)

    is_g = (col >= N_EXPERTS) & (col < N_EXPERTS + N_GROUPS)
    lgm = jnp.where(is_g, logit, neg)
    mg = jnp.max(lgm, axis=1, keepdims=True)
    pg = 1.0 / jnp.sum(jnp.exp(lgm - mg), axis=1, keepdims=True)
    gi = first_at(is_g & (logit == mg)) - float(N_EXPERTS)
    is_e = (col < N_EXPERTS) & ((col // EXP_PER_GROUP).astype(F32) == gi)
    lem = jnp.where(is_e, logit, neg)
    ee = jnp.exp(lem - jnp.max(lem, axis=1, keepdims=True))
    pe = ee / jnp.sum(ee, axis=1, keepdims=True)
    p1 = jnp.max(jnp.where(is_e, pe, -1.0), axis=1, keepdims=True)
    i1 = first_at(is_e & (pe == p1))
    rest = is_e & (colf != i1)
    p2 = jnp.max(jnp.where(rest, pe, -1.0), axis=1, keepdims=True)
    i2 = first_at(rest & (pe == p2))
    den = p1 + p2

    r_i = lax.broadcasted_iota(jnp.int32, (tm, tm), 0)
    c_i = lax.broadcasted_iota(jnp.int32, (tm, tm), 1)
    earlier = (c_i < r_i).astype(BF16)
    oh1 = (colf == i1).astype(F32)
    oh2 = (colf == i2).astype(F32)
    run = run_s[...]
    rank1 = jnp.sum(oh1 * (_bdot(earlier, oh1) + run), axis=1, keepdims=True)
    run = run + jnp.sum(oh1, axis=0, keepdims=True)
    rank2 = jnp.sum(oh2 * (_bdot(earlier, oh2) + run), axis=1, keepdims=True)
    run = run + jnp.sum(oh2, axis=0, keepdims=True)
    run_s[...] = run
    cnt_ref[...] = run
    vals = (i1, i2, pg * p1 / den, pg * p2 / den, rank1, rank2)
    route_ref[0] = sum(jnp.where(col == k, v, 0.0) for k, v in enumerate(vals))


def _merge_call(x, ya, z, mb, gate, shift, scale, norm2_g, w_proj_a, w_out, w_router, b_router):
    bsz, seq, d = x.shape
    tm = min(seq, 512)
    z4 = z.reshape(z.shape[0], bsz, seq, d)
    tok = pl.BlockSpec((1, tm, d), lambda b, i: (b, i, 0))

    def slab(j):
        return pl.BlockSpec((1, 1, tm, d), lambda b, i: (j, b, i, 0))

    vec = pl.BlockSpec((1, 1, d), lambda b, i: (b, 0, 0))
    row = pl.BlockSpec((1, d), lambda b, i: (0, 0))
    wsp = pl.BlockSpec((d, d), lambda b, i: (0, 0))
    return pl.pallas_call(
        _merge_kernel,
        grid=(bsz, seq // tm),
        in_specs=[tok, tok, slab(1), slab(2), tok,
                  vec, vec, vec, row, wsp, wsp,
                  pl.BlockSpec((d, LANES), lambda b, i: (0, 0)),
                  pl.BlockSpec((1, LANES), lambda b, i: (0, 0))],
        out_specs=[tok, tok, pl.BlockSpec((1, tm, LANES), lambda b, i: (b, i, 0)),
                   pl.BlockSpec((1, LANES), lambda b, i: (0, 0))],
        out_shape=[jax.ShapeDtypeStruct((bsz, seq, d), F32),
                   jax.ShapeDtypeStruct((bsz, seq, d), F32),
                   jax.ShapeDtypeStruct((bsz, seq, LANES), F32),
                   jax.ShapeDtypeStruct((1, LANES), F32)],
        scratch_shapes=[pltpu.VMEM((1, LANES), F32)],
        compiler_params=_params("arbitrary", "arbitrary"),
        name="merge",
    )(x, ya, z4, z4, mb, gate, shift, scale, norm2_g.reshape(1, d),
      w_proj_a.astype(BF16), w_out.astype(BF16), w_router, b_router)


MOE_TILE = 256


def _row_copy(src_ref, src_row, dst_ref, dst_row, sem):
    return pltpu.make_async_copy(src_ref.at[pl.ds(src_row, 1), :], dst_ref.at[pl.ds(dst_row, 1), :], sem)


def _dispatch_kernel(pos_ref, h_ref, xs_in_ref, xs_ref, sem):
    del xs_in_ref
    tm = h_ref.shape[0]

    def copies(r):
        return [_row_copy(h_ref, r, xs_ref, pos_ref[0, 0, k * tm + r], sem) for k in range(2)]

    def start(r, carry):
        for cp in copies(r):
            cp.start()
        return carry

    def wait(r, carry):
        for cp in copies(r):
            cp.wait()
        return carry

    lax.fori_loop(0, tm, start, 0)
    lax.fori_loop(0, tm, wait, 0)


def _dispatch_call(pos, h2, n_slots):
    t, d = h2.shape
    tm = pos.shape[2] // 2
    return pl.pallas_call(
        _dispatch_kernel,
        grid=(t // tm,),
        in_specs=[pl.BlockSpec((1, 1, 2 * tm), lambda i: (i, 0, 0), memory_space=pltpu.SMEM),
                  pl.BlockSpec((tm, d), lambda i: (i, 0)),
                  pl.BlockSpec(memory_space=pltpu.HBM)],
        out_specs=pl.BlockSpec(memory_space=pltpu.HBM),
        out_shape=jax.ShapeDtypeStruct((n_slots, d), F32),
        scratch_shapes=[pltpu.SemaphoreType.DMA(())],
        input_output_aliases={2: 0},
        compiler_params=_params("arbitrary"),
        name="dispatch",
    )(pos, h2, jnp.zeros((n_slots, d), F32))


def _experts_kernel(te_ref, nu_ref, xs_ref, w1_ref, w3_ref, w2_ref, ys_ref, w1_s, w3_s, w2_s):
    i = pl.program_id(0)

    @pl.when(i < nu_ref[0])
    def _():
        @pl.when((i == 0) | (te_ref[i] != te_ref[jnp.maximum(i - 1, 0)]))
        def _():
            w1_s[...] = w1_ref[0].astype(BF16)
            w3_s[...] = w3_ref[0].astype(BF16)
            w2_s[...] = w2_ref[0].astype(BF16)

        xb = xs_ref[...].astype(BF16)
        up = _silu(_bdot(xb, w1_s[...])) * _bdot(xb, w3_s[...])
        ys_ref[...] = _bdot(up, w2_s[...])

    @pl.when(i >= nu_ref[0])
    def _():
        ys_ref[...] = jnp.zeros_like(ys_ref)


def _experts_call(tile_expert, n_used, xs, w1, w3, w2):
    n_slots, d = xs.shape
    _, _, d_e = w1.shape
    n_tiles = n_slots // MOE_TILE
    tok = pl.BlockSpec((MOE_TILE, d), lambda i, te, nu: (jnp.minimum(i, nu[0] - 1), 0))
    return pl.pallas_call(
        _experts_kernel,
        grid_spec=pltpu.PrefetchScalarGridSpec(
            num_scalar_prefetch=2,
            grid=(n_tiles,),
            in_specs=[tok,
                      pl.BlockSpec((1, d, d_e), lambda i, te, nu: (te[i], 0, 0)),
                      pl.BlockSpec((1, d, d_e), lambda i, te, nu: (te[i], 0, 0)),
                      pl.BlockSpec((1, d_e, d), lambda i, te, nu: (te[i], 0, 0))],
            out_specs=pl.BlockSpec((MOE_TILE, d), lambda i, te, nu: (i, 0)),
            scratch_shapes=[pltpu.VMEM((d, d_e), BF16), pltpu.VMEM((d, d_e), BF16), pltpu.VMEM((d_e, d), BF16)]),
        out_shape=jax.ShapeDtypeStruct((n_slots, d), F32),
        compiler_params=_params("arbitrary"),
        name="experts",
    )(tile_expert, n_used, xs, w1, w3, w2)


def _combine_kernel(pos_ref, route_ref, x1_ref, gate_ref, fg_ref, ys_ref, o_ref, y1_s, y2_s, sem):
    tm = x1_ref.shape[0]

    def copies(r):
        return [_row_copy(ys_ref, pos_ref[0, 0, r], y1_s, r, sem),
                _row_copy(ys_ref, pos_ref[0, 0, tm + r], y2_s, r, sem)]

    def start(r, carry):
        for cp in copies(r):
            cp.start()
        return carry

    def wait(r, carry):
        for cp in copies(r):
            cp.wait()
        return carry

    lax.fori_loop(0, tm, start, 0)
    lax.fori_loop(0, tm, wait, 0)
    route = route_ref[...]
    y = route[:, 2:3] * y1_s[...] + route[:, 3:4] * y2_s[...]
    x2 = x1_ref[...] + gate_ref[0] * y
    o_ref[...] = x2 * lax.rsqrt(jnp.mean(x2 * x2, axis=-1, keepdims=True) + EPS) * fg_ref[...]


def _combine_call(pos, route, x1, gate, final_g, ys, seq):
    t, d = x1.shape
    tm = pos.shape[2] // 2
    per_b = seq // tm
    tok = pl.BlockSpec((tm, d), lambda i: (i, 0))
    return pl.pallas_call(
        _combine_kernel,
        grid=(t // tm,),
        in_specs=[pl.BlockSpec((1, 1, 2 * tm), lambda i: (i, 0, 0), memory_space=pltpu.SMEM),
                  pl.BlockSpec((tm, LANES), lambda i: (i, 0)), tok,
                  pl.BlockSpec((1, 1, d), lambda i: (i // per_b, 0, 0)),
                  pl.BlockSpec((1, d), lambda i: (0, 0)),
                  pl.BlockSpec(memory_space=pltpu.HBM)],
        out_specs=tok,
        out_shape=jax.ShapeDtypeStruct((t, d), F32),
        scratch_shapes=[pltpu.VMEM((tm, d), F32), pltpu.VMEM((tm, d), F32), pltpu.SemaphoreType.DMA(())],
        compiler_params=_params("arbitrary"),
        name="combine",
    )(pos, route, x1, gate, final_g.reshape(1, d), ys)


def _moe_plan(route, cnt, tm):
    t = route.shape[0]
    counts = cnt[0, :N_EXPERTS].astype(jnp.int32)
    n_tiles_e = (counts + MOE_TILE - 1) // MOE_TILE
    tile_end = jnp.cumsum(n_tiles_e)
    offsets = (tile_end - n_tiles_e) * MOE_TILE
    n_used = tile_end[-1]
    experts = jnp.arange(N_EXPERTS, dtype=jnp.int32)

    def slot(eid, rank):
        base = jnp.sum(jnp.where(eid.astype(jnp.int32)[:, None] == experts[None, :], offsets[None, :], 0), axis=1)
        return base + rank.astype(jnp.int32)

    pos1 = slot(route[:, 0], route[:, 4]).reshape(t // tm, tm)
    pos2 = slot(route[:, 1], route[:, 5]).reshape(t // tm, tm)
    pos = jnp.concatenate([pos1, pos2], axis=1)[:, None, :]
    n_tiles = (2 * t) // MOE_TILE + N_EXPERTS
    tiles = jnp.arange(n_tiles, dtype=jnp.int32)
    tile_expert = jnp.sum(jnp.minimum(tiles, n_used - 1)[:, None] >= tile_end[None, :], axis=1).astype(jnp.int32)
    return pos, tile_expert, n_used.reshape(1).astype(jnp.int32), n_tiles * MOE_TILE


def kernel(x, c, ctx, c_ctx, w_mod, b_mod, norm1_g, w_in, conv_w, conv_b, lru_wa, lru_ba, lru_wx, lru_bx,
           lru_lam, hg_lb, hg_norm_g, w_proj_a, w_proj_b, w_out, norm2_g, w_rg, b_rg, w_re, b_re,
           w1, w3, w2, final_g):
    bsz, seq, d = x.shape
    seq_c = ctx.shape[1]
    l = 0

    pad = (-(bsz + 1)) % SUBLANES
    cc = jnp.concatenate([c, c_ctx[None, :], jnp.zeros((pad, d), F32)], axis=0)
    mod = _mod_call(cc, w_mod[l], b_mod[l])
    mx = [mod[:bsz, None, k * d:(k + 1) * d] for k in range(6)]
    mc = [jnp.broadcast_to(mod[bsz:bsz + 1, None, k * d:(k + 1) * d], (bsz, 1, d)) for k in range(2)]
    lb = jax.nn.softmax(hg_lb.astype(F32), axis=0)[l]

    hx = _norm_call(x, norm1_g[l], mx[0], mx[1])
    hc = _norm_call(ctx, norm1_g[l], mc[0], mc[1]).reshape(bsz * seq_c, d)
    rows = seq // GRID_W
    hx_cm = hx.reshape(bsz, rows, GRID_W, d).swapaxes(1, 2).reshape(bsz * seq, d)
    hx = hx.reshape(bsz * seq, d)
    z = _proj_call(hx, w_in[l], lambda j: jnp.where(j < 2, j, 7), 3, "proj_r")
    zh = _proj_call(hx_cm, w_in[l], lambda j: jnp.where(j < 5, j + 2, 8), 6, "proj_h")
    zc = _proj_call(hc, w_in[l], lambda j: jnp.where(j > 0, j + 2, 0), 4, "proj_c")

    ya = _lru_call(z, zc, conv_w[l], conv_b[l], lru_wa[l], lru_wx[l], lru_ba[l], lru_bx[l], lru_lam[l], bsz)
    s_f, s_b = _hgc_call(zc, lb, bsz)
    o_f = _hg_call(zh, 1, lb, s_f, bsz, reverse=False)
    mb_cm = _hg_call(zh, 2, lb, s_b, bsz, reverse=True, post=(o_f, hg_norm_g[l], w_proj_b[l]))
    mb = mb_cm.reshape(bsz, GRID_W, rows, d).swapaxes(1, 2).reshape(bsz, seq, d)

    w_router = jnp.zeros((d, LANES), F32).at[:, :N_EXPERTS].set(w_re[l])
    w_router = w_router.at[:, N_EXPERTS:N_EXPERTS + N_GROUPS].set(w_rg[l])
    b_router = jnp.zeros((1, LANES), F32).at[0, :N_EXPERTS].set(b_re[l])
    b_router = b_router.at[0, N_EXPERTS:N_EXPERTS + N_GROUPS].set(b_rg[l])
    x1, h2, route, cnt = _merge_call(x, ya, z, mb, mx[2], mx[3], mx[4], norm2_g[l],
                                     w_proj_a[l], w_out[l], w_router, b_router)

    route = route.reshape(bsz * seq, LANES)
    pos, tile_expert, n_used, n_slots = _moe_plan(route, cnt, min(seq, 512))
    xs = _dispatch_call(pos, h2.reshape(bsz * seq, d), n_slots)
    ys = _experts_call(tile_expert, n_used, xs, w1[l], w3[l], w2[l])
    out = _combine_call(pos, route, x1.reshape(bsz * seq, d), mx[5], final_g, ys, seq)
    return out.reshape(bsz, seq, d)
```

```python
import functools

import jax
import jax.numpy as jnp
from jax import lax
from jax.experimental import pallas as pl
from jax.experimental.pallas import tpu as pltpu

F32 = jnp.float32
BF16 = jnp.bfloat16
HIGHEST = lax.Precision.HIGHEST

EPS = 1e-6
GRID_W = 64
CHUNK = 32
LRU_C = 8.0
CONV_W = 4
N_HEADS = 8
HEAD_DIM = 128
N_GROUPS = 4
EXP_PER_GROUP = 8
N_EXPERTS = N_GROUPS * EXP_PER_GROUP
LANES = 128
SUBLANES = 8
VMEM_LIMIT = 56 * 1024 * 1024


def _params(*sem):
    return pltpu.CompilerParams(dimension_semantics=sem, vmem_limit_bytes=VMEM_LIMIT)


def _silu(x):
    return x * jax.nn.sigmoid(x)


def _bdot(a, b):
    return jnp.dot(a.astype(BF16), b.astype(BF16), preferred_element_type=F32)


def _mod_kernel(c_ref, w_ref, b_ref, o_ref):
    o_ref[...] = jnp.dot(_silu(c_ref[...]), w_ref[...], precision=HIGHEST,
                         preferred_element_type=F32) + b_ref[...]


def _mod_call(cc, w_mod, b_mod):
    rows, d = cc.shape
    n = w_mod.shape[1]
    tn = 1024
    return pl.pallas_call(
        _mod_kernel,
        grid=(n // tn,),
        in_specs=[pl.BlockSpec((rows, d), lambda j: (0, 0)),
                  pl.BlockSpec((d, tn), lambda j: (0, j)),
                  pl.BlockSpec((1, tn), lambda j: (0, j))],
        out_specs=pl.BlockSpec((rows, tn), lambda j: (0, j)),
        out_shape=jax.ShapeDtypeStruct((rows, n), F32),
        compiler_params=_params("arbitrary"),
        name="mod",
    )(cc, w_mod, b_mod.reshape(1, n))


def _norm_mod(x, g, shift, scale):
    y = x * lax.rsqrt(jnp.mean(x * x, axis=-1, keepdims=True) + EPS) * g
    return y * (1.0 + scale) + shift


def _norm_kernel(x_ref, g_ref, shift_ref, scale_ref, o_ref):
    o_ref[0] = _norm_mod(x_ref[0], g_ref[...], shift_ref[0], scale_ref[0]).astype(o_ref.dtype)


def _norm_call(x, g, shift, scale):
    b, l, d = x.shape
    tl = min(l, 512)
    vec = pl.BlockSpec((1, 1, d), lambda i, j: (i, 0, 0))
    return pl.pallas_call(
        _norm_kernel,
        grid=(b, l // tl),
        in_specs=[pl.BlockSpec((1, tl, d), lambda i, j: (i, j, 0)),
                  pl.BlockSpec((1, d), lambda i, j: (0, 0)), vec, vec],
        out_specs=pl.BlockSpec((1, tl, d), lambda i, j: (i, j, 0)),
        out_shape=jax.ShapeDtypeStruct((b, l, d), BF16),
        compiler_params=_params("arbitrary", "arbitrary"),
        name="norm",
    )(x, g.reshape(1, d), shift, scale)


def _proj_kernel(a_ref, w_ref, o_ref):
    o_ref[0] = jnp.dot(a_ref[...], w_ref[...].astype(BF16), preferred_element_type=F32)


def _proj_call(a, w, col_of, n_out, name):
    t, k = a.shape
    tn = 1024
    tm = min(t, 2048)
    return pl.pallas_call(
        _proj_kernel,
        grid=(t // tm, n_out),
        in_specs=[pl.BlockSpec((tm, k), lambda i, j: (i, 0)),
                  pl.BlockSpec((k, tn), lambda i, j: (0, col_of(j)))],
        out_specs=pl.BlockSpec((1, tm, tn), lambda i, j: (j, i, 0)),
        out_shape=jax.ShapeDtypeStruct((n_out, t, tn), F32),
        compiler_params=_params("arbitrary", "arbitrary"),
        name=name,
    )(a, w)


def _shift_rows(x, k):
    n = x.shape[0]
    row = lax.broadcasted_iota(jnp.int32, x.shape, 0)
    rolled = pltpu.roll(x, k % n, 0)
    keep = (row >= k) if k > 0 else (row < n + k)
    return jnp.where(keep, rolled, 0.0)


def _lru_kernel(ax_ref, axc_ref, cw_ref, cb_ref, wa_ref, wx_ref, ba_ref, bx_ref, lam_ref,
                o_ref, a_s, b_s):
    seq, width = ax_ref.shape[1], ax_ref.shape[2]
    seq_c = axc_ref.shape[1]
    heads = width // HEAD_DIM

    def conv(x):
        acc = _shift_rows(x, 2) * cw_ref[0:1, :]
        acc = acc + _shift_rows(x, 1) * cw_ref[1:2, :]
        acc = acc + x * cw_ref[2:3, :]
        acc = acc + _shift_rows(x, -1) * cw_ref[3:4, :]
        return acc + cb_ref[...]

    def coeffs(u, d, h):
        hs = slice(h * HEAD_DIM, (h + 1) * HEAD_DIM)
        ub = u.astype(BF16)
        r = jax.nn.sigmoid(_bdot(ub, wa_ref[d, h]) + ba_ref[d:d + 1, hs])
        i = jax.nn.sigmoid(_bdot(ub, wx_ref[d, h]) + bx_ref[d:d + 1, hs])
        log_a = -LRU_C * r * jax.nn.softplus(-lam_ref[d:d + 1, hs])
        a = jnp.exp(log_a)
        return a, jnp.sqrt(1.0 - a * a) * (i * u)

    u_c = conv(axc_ref[0])
    u_x = conv(ax_ref[0])
    n_blk = seq // SUBLANES
    row8 = lax.broadcasted_iota(jnp.int32, (SUBLANES, width), 0)

    for d, reverse in enumerate((False, True)):
        edge = u_c[seq_c - SUBLANES:] if reverse else u_c[:SUBLANES]
        pick = SUBLANES - 1 if reverse else 0
        h0 = jnp.concatenate([coeffs(edge[:, h * HEAD_DIM:(h + 1) * HEAD_DIM], d, h)[1][pick:pick + 1]
                              for h in range(heads)], axis=1)
        for h in range(heads):
            hs = slice(h * HEAD_DIM, (h + 1) * HEAD_DIM)
            a, b = coeffs(u_x[:, hs], d, h)
            a_s[:, hs] = a
            b_s[:, hs] = b

        def body(i, h_prev, reverse=reverse):
            blk = (n_blk - 1 - i) if reverse else i
            rows = pl.ds(pl.multiple_of(blk * SUBLANES, SUBLANES), SUBLANES)
            a = a_s[rows, :]
            b = b_s[rows, :]
            for s in (1, 2, 4):
                if reverse:
                    keep = row8 < SUBLANES - s
                    sh = SUBLANES - s
                else:
                    keep = row8 >= s
                    sh = s
                a_sh = jnp.where(keep, pltpu.roll(a, sh, 0), 1.0)
                b_sh = jnp.where(keep, pltpu.roll(b, sh, 0), 0.0)
                b = a * b_sh + b
                a = a * a_sh
            hh = b + a * h_prev
            if reverse:
                o_ref[0, rows, :] = o_ref[0, rows, :] + hh
                return hh[0:1, :]
            o_ref[0, rows, :] = hh
            return hh[SUBLANES - 1:SUBLANES, :]

        lax.fori_loop(0, n_blk, body, h0, unroll=4)


def _lru_call(z, zc, conv_w, conv_b, wa, wx, ba, bx, lam, bsz):
    t = z.shape[1]
    seq = t // bsz
    seq_c = zc.shape[1] // bsz
    width = 2 * HEAD_DIM
    n_w = (N_HEADS * HEAD_DIM) // width
    z4 = z.reshape(z.shape[0], bsz, seq, z.shape[2])
    zc4 = zc.reshape(zc.shape[0], bsz, seq_c, zc.shape[2])
    vec2 = pl.BlockSpec((2, width), lambda b, w: (0, w))
    wspec = pl.BlockSpec((2, width // HEAD_DIM, HEAD_DIM, HEAD_DIM), lambda b, w: (0, w, 0, 0))

    def kern(ax_ref, axc_ref, *rest):
        _lru_kernel(ax_ref.at[0], axc_ref.at[0], *rest)

    return pl.pallas_call(
        kern,
        grid=(bsz, n_w),
        in_specs=[pl.BlockSpec((1, 1, seq, width), lambda b, w: (0, b, 0, w)),
                  pl.BlockSpec((1, 1, seq_c, width), lambda b, w: (0, b, 0, w)),
                  pl.BlockSpec((CONV_W, width), lambda b, w: (0, w)),
                  pl.BlockSpec((1, width), lambda b, w: (0, w)),
                  wspec, wspec, vec2, vec2, vec2],
        out_specs=pl.BlockSpec((1, seq, width), lambda b, w: (b, 0, w)),
        out_shape=jax.ShapeDtypeStruct((bsz, seq, N_HEADS * HEAD_DIM), F32),
        scratch_shapes=[pltpu.VMEM((seq, width), F32), pltpu.VMEM((seq, width), F32)],
        compiler_params=_params("arbitrary", "arbitrary"),
        name="lru",
    )(z4, zc4, conv_w, conv_b.reshape(1, -1), wa, wx, ba, bx, lam)


def _split3(x):
    x1 = x.astype(BF16)
    r1 = x - x1.astype(F32)
    x2 = r1.astype(BF16)
    x3 = (r1 - x2.astype(F32)).astype(BF16)
    return x1, x2, x3


def _tri_cumsum(tri, x):
    x1, x2, x3 = _split3(x)
    dot = functools.partial(jnp.dot, preferred_element_type=F32)
    return dot(tri, x1) + dot(tri, x2) + dot(tri, x3)


def _hg_gates(f, lb):
    g = lb + (1.0 - lb) * jax.nn.sigmoid(f)
    return jnp.log(g), 1.0 - g


def _dot_tn(a, b):
    return lax.dot_general(a.astype(BF16), b.astype(BF16), (((0,), (0,)), ((), ())),
                           preferred_element_type=F32)


def _dot_nt(a, b):
    return lax.dot_general(a.astype(BF16), b.astype(BF16), (((1,), (1,)), ((), ())),
                           preferred_element_type=F32)


def _hgc_kernel(ff_ref, fb_ref, v_ref, lb_ref, sf_ref, sb_ref):
    n = ff_ref.shape[1]
    row = lax.broadcasted_iota(jnp.int32, (n, n), 0)
    col = lax.broadcasted_iota(jnp.int32, (n, n), 1)
    tri = (col <= row).astype(BF16)
    lb = lb_ref[...]
    v = v_ref[0]
    lg, k = _hg_gates(ff_ref[0], lb)
    lc = _tri_cumsum(tri, lg)
    kf = k * jnp.exp(lc[n - 1:n, :] - lc)
    lg, k = _hg_gates(fb_ref[0], lb)
    lc = _tri_cumsum(tri, lg)
    kb = k * jnp.exp(lc - lg)
    for h in range(N_HEADS):
        hs = slice(h * HEAD_DIM, (h + 1) * HEAD_DIM)
        sf_ref[0, h] = _dot_tn(v[:, hs], kf[:, hs])
        sb_ref[0, h] = _dot_tn(v[:, hs], kb[:, hs])


def _hgc_call(zc, lb, bsz):
    seq_c = zc.shape[1] // bsz
    width = zc.shape[2]
    zc4 = zc.reshape(zc.shape[0], bsz, seq_c, width)

    def slab(j):
        return pl.BlockSpec((1, 1, seq_c, width), lambda b: (j, b, 0, 0))

    def kern(ff_ref, fb_ref, v_ref, lb_ref, sf_ref, sb_ref):
        _hgc_kernel(ff_ref.at[0], fb_ref.at[0], v_ref.at[0], lb_ref, sf_ref, sb_ref)

    sspec = pl.BlockSpec((1, N_HEADS, HEAD_DIM, HEAD_DIM), lambda b: (b, 0, 0, 0))
    sshape = jax.ShapeDtypeStruct((bsz, N_HEADS, HEAD_DIM, HEAD_DIM), F32)
    return pl.pallas_call(
        kern,
        grid=(bsz,),
        in_specs=[slab(1), slab(2), slab(3), pl.BlockSpec((1, width), lambda b: (0, 0))],
        out_specs=[sspec, sspec],
        out_shape=[sshape, sshape],
        compiler_params=_params("arbitrary"),
        name="hgc",
    )(zc4, zc4, zc4, lb.reshape(1, width))


def _hg_kernel(q_ref, f_ref, v_ref, lb_ref, o_ref, st_s, *, reverse, n_chunks, post):
    width = N_HEADS * HEAD_DIM
    rows = n_chunks * CHUNK

    row = lax.broadcasted_iota(jnp.int32, (rows, rows), 0)
    col = lax.broadcasted_iota(jnp.int32, (rows, rows), 1)
    same = (row // CHUNK) == (col // CHUNK)
    causal = same & ((col >= row) if reverse else (col <= row))
    last, mid = (0, CHUNK - 1 - CHUNK // 2) if reverse else (CHUNK - 1, CHUNK // 2)

    def chunked(t):
        return t.reshape(n_chunks, CHUNK, width)

    def flat(t):
        return t.reshape(rows, width)

    def chunk_cumsum(x):
        pos = lax.broadcasted_iota(jnp.int32, x.shape, 0) % CHUNK
        s = 1
        while s < CHUNK:
            if reverse:
                x = x + jnp.where(pos < CHUNK - s, pltpu.roll(x, rows - s, 0), 0.0)
            else:
                x = x + jnp.where(pos >= s, pltpu.roll(x, s, 0), 0.0)
            s *= 2
        return x

    q = chunked(_silu(q_ref[0]))
    v = v_ref[0]
    lg, k = _hg_gates(f_ref[0], lb_ref[...])
    k = chunked(k)
    lc = chunked(chunk_cumsum(lg))
    ltot = lc[:, last:last + 1, :]
    lmid = lc[:, mid:mid + 1, :]
    e_tot = jnp.exp(ltot)
    q_in = flat(q * jnp.exp(lc))
    q_at = flat(q * jnp.exp(lc - lmid))
    k_at = flat(k * jnp.exp(lmid - lc))
    k_st = flat(k * jnp.exp(ltot - lc))

    heads = [slice(h * HEAD_DIM, (h + 1) * HEAD_DIM) for h in range(N_HEADS)]
    chunks = [slice(c * CHUNK, (c + 1) * CHUNK) for c in range(n_chunks)]
    order = list(reversed(range(n_chunks))) if reverse else list(range(n_chunks))
    att = [jnp.where(causal, _dot_nt(q_at[:, hs], k_at[:, hs]), 0.0) for hs in heads]
    o_intra = [_bdot(att[h], v[:, hs]) for h, hs in enumerate(heads)]
    d_st = [[_dot_tn(v[rs, hs], k_st[rs, hs]) for rs in chunks] for hs in heads]
    outs = []
    for h, hs in enumerate(heads):
        st = st_s[h]
        st_in = [None] * n_chunks
        for c in order:
            st_in[c] = st
            st = st * e_tot[c, :, hs] + d_st[h][c]
        st_s[h] = st
        o_inter = [_dot_nt(q_in[rs, hs], st_in[c]) for c, rs in enumerate(chunks)]
        outs.append(o_intra[h] + jnp.concatenate(o_inter, axis=0))

    if post is None:
        for h in range(N_HEADS):
            o_ref[0, :, h * HEAD_DIM:(h + 1) * HEAD_DIM] = outs[h]
        return
    of_ref, og_ref, gb_ref, hgn_ref, wpb_ref = post
    parts = []
    for h in range(N_HEADS):
        hs = slice(h * HEAD_DIM, (h + 1) * HEAD_DIM)
        o = of_ref[0, :, hs] + outs[h]
        parts.append(o * lax.rsqrt(jnp.mean(o * o, axis=-1, keepdims=True) + EPS) * hgn_ref[:, hs])
    yb = jnp.concatenate(parts, axis=1)
    o_ref[0] = jax.nn.sigmoid(gb_ref[0]) * _bdot(yb * _silu(og_ref[0]), wpb_ref[...])


def _hg_call(zh, f_slab, lb, s0, bsz, reverse, post=None):
    t, width = zh.shape[1], zh.shape[2]
    seq = t // bsz
    assert seq // GRID_W == CHUNK
    n_chunks = 8
    rows = n_chunks * CHUNK
    n_grp = seq // rows
    z4 = zh.reshape(zh.shape[0], bsz, seq, width)

    def grp(g):
        return (n_grp - 1 - g) if reverse else g

    nb = 2 if bsz % 2 == 0 else 1

    def slab(j):
        return pl.BlockSpec((1, nb, rows, width), lambda b, g: (j, b, grp(g), 0))

    tok = pl.BlockSpec((nb, rows, width), lambda b, g: (b, grp(g), 0))
    row = pl.BlockSpec((1, width), lambda b, g: (0, 0))
    in_specs = [slab(0), slab(f_slab), slab(3), row,
                pl.BlockSpec((nb, N_HEADS, HEAD_DIM, HEAD_DIM), lambda b, g: (b, 0, 0, 0))]
    args = [z4, z4, z4, lb.reshape(1, width), s0]
    if post is not None:
        o_other, hg_norm_g, w_proj_b = post
        in_specs += [tok, slab(4), slab(5), row, pl.BlockSpec((width, width), lambda b, g: (0, 0))]
        args += [o_other, z4, z4, hg_norm_g.reshape(1, width), w_proj_b.astype(BF16)]

    def kern(q_ref, f_ref, v_ref, lb_ref, s0_ref, *rest):
        *extra, o_ref, st_s = rest

        @pl.when(pl.program_id(1) == 0)
        def _():
            st_s[...] = s0_ref[...]

        for bi in range(nb):
            one = pl.ds(bi, 1)
            post_refs = None
            if extra:
                of_ref, og_ref, gb_ref, hgn_ref, wpb_ref = extra
                post_refs = (of_ref.at[one], og_ref.at[0, one], gb_ref.at[0, one], hgn_ref, wpb_ref)
            _hg_kernel(q_ref.at[0, one], f_ref.at[0, one], v_ref.at[0, one], lb_ref, o_ref.at[one],
                       st_s.at[bi], reverse=reverse, n_chunks=n_chunks, post=post_refs)

    return pl.pallas_call(
        kern,
        grid=(bsz // nb, n_grp),
        in_specs=in_specs,
        out_specs=tok,
        out_shape=jax.ShapeDtypeStruct((bsz, seq, width), F32),
        scratch_shapes=[pltpu.VMEM((nb, N_HEADS, HEAD_DIM, HEAD_DIM), F32)],
        compiler_params=_params("arbitrary", "arbitrary"),
        name="hg_bwd" if reverse else "hg_fwd",
    )(*args)


def _merge_kernel(x_ref, ya_ref, ay_ref, ga_ref, mb_ref,
                  gate_ref, shift_ref, scale_ref, n2_ref,
                  wpa_ref, wo_ref, wr_ref, br_ref,
                  x1_ref, h2_ref, route_ref, cnt_ref, run_s):
    tm = x_ref.shape[1]

    @pl.when((pl.program_id(0) == 0) & (pl.program_id(1) == 0))
    def _():
        run_s[...] = jnp.zeros_like(run_s)

    pa = _bdot(ya_ref[0] * jax.nn.gelu(ay_ref[0, 0]), wpa_ref[...])
    merged = jax.nn.sigmoid(ga_ref[0, 0]) * pa + mb_ref[0]
    x1 = x_ref[0] + gate_ref[0] * _bdot(merged, wo_ref[...])
    x1_ref[0] = x1
    h2 = _norm_mod(x1, n2_ref[...], shift_ref[0], scale_ref[0])
    h2_ref[0] = h2

    logit = jnp.dot(h2, wr_ref[...], precision=HIGHEST, preferred_element_type=F32) + br_ref[...]
    col = lax.broadcasted_iota(jnp.int32, (tm, LANES), 1)
    colf = col.astype(F32)
    neg = -jnp.inf

    def first_at(mask):
        return -jnp.max(jnp.where(mask, -colf, -1e9), axis=1, keepdims=True)

    is_g = (col >= N_EXPERTS) & (col < N_EXPERTS + N_GROUPS)
    lgm = jnp.where(is_g, logit, neg)
    mg = jnp.max(lgm, axis=1, keepdims=True)
    pg = 1.0 / jnp.sum(jnp.exp(lgm - mg), axis=1, keepdims=True)
    gi = first_at(is_g & (logit == mg)) - float(N_EXPERTS)
    is_e = (col < N_EXPERTS) & ((col // EXP_PER_GROUP).astype(F32) == gi)
    lem = jnp.where(is_e, logit, neg)
    ee = jnp.exp(lem - jnp.max(lem, axis=1, keepdims=True))
    pe = ee / jnp.sum(ee, axis=1, keepdims=True)
    p1 = jnp.max(jnp.where(is_e, pe, -1.0), axis=1, keepdims=True)
    i1 = first_at(is_e & (pe == p1))
    rest = is_e & (colf != i1)
    p2 = jnp.max(jnp.where(rest, pe, -1.0), axis=1, keepdims=True)
    i2 = first_at(rest & (pe == p2))
    den = p1 + p2

    r_i = lax.broadcasted_iota(jnp.int32, (tm, tm), 0)
    c_i = lax.broadcasted_iota(jnp.int32, (tm, tm), 1)
    earlier = (c_i < r_i).astype(BF16)
    oh1 = (colf == i1).astype(F32)
    oh2 = (colf == i2).astype(F32)
    run = run_s[...]
    rank1 = jnp.sum(oh1 * (_bdot(earlier, oh1) + run), axis=1, keepdims=True)
    run = run + jnp.sum(oh1, axis=0, keepdims=True)
    rank2 = jnp.sum(oh2 * (_bdot(earlier, oh2) + run), axis=1, keepdims=True)
    run = run + jnp.sum(oh2, axis=0, keepdims=True)
    run_s[...] = run
    cnt_ref[...] = run
    vals = (i1, i2, pg * p1 / den, pg * p2 / den, rank1, rank2)
    route_ref[0] = sum(jnp.where(col == k, v, 0.0) for k, v in enumerate(vals))


def _merge_call(x, ya, z, mb, gate, shift, scale, norm2_g, w_proj_a, w_out, w_router, b_router):
    bsz, seq, d = x.shape
    tm = min(seq, 512)
    z4 = z.reshape(z.shape[0], bsz, seq, d)
    tok = pl.BlockSpec((1, tm, d), lambda b, i: (b, i, 0))

    def slab(j):
        return pl.BlockSpec((1, 1, tm, d), lambda b, i: (j, b, i, 0))

    vec = pl.BlockSpec((1, 1, d), lambda b, i: (b, 0, 0))
    row = pl.BlockSpec((1, d), lambda b, i: (0, 0))
    wsp = pl.BlockSpec((d, d), lambda b, i: (0, 0))
    return pl.pallas_call(
        _merge_kernel,
        grid=(bsz, seq // tm),
        in_specs=[tok, tok, slab(1), slab(2), tok,
                  vec, vec, vec, row, wsp, wsp,
                  pl.BlockSpec((d, LANES), lambda b, i: (0, 0)),
                  pl.BlockSpec((1, LANES), lambda b, i: (0, 0))],
        out_specs=[tok, tok, pl.BlockSpec((1, tm, LANES), lambda b, i: (b, i, 0)),
                   pl.BlockSpec((1, LANES), lambda b, i: (0, 0))],
        out_shape=[jax.ShapeDtypeStruct((bsz, seq, d), F32),
                   jax.ShapeDtypeStruct((bsz, seq, d), F32),
                   jax.ShapeDtypeStruct((bsz, seq, LANES), F32),
                   jax.ShapeDtypeStruct((1, LANES), F32)],
        scratch_shapes=[pltpu.VMEM((1, LANES), F32)],
        compiler_params=_params("arbitrary", "arbitrary"),
        name="merge",
    )(x, ya, z4, z4, mb, gate, shift, scale, norm2_g.reshape(1, d),
      w_proj_a.astype(BF16), w_out.astype(BF16), w_router, b_router)


MOE_TILE = 256


def _row_copy(src_ref, src_row, dst_ref, dst_row, sem):
    return pltpu.make_async_copy(src_ref.at[pl.ds(src_row, 1), :], dst_ref.at[pl.ds(dst_row, 1), :], sem)


ROW_DMA_UNROLL = 8


def _start_and_wait_rows(copies, n_rows):
    def start(r, carry):
        for k, cp in enumerate(copies(r)):
            cp.start(priority=k)
        return carry

    def wait(r, carry):
        for cp in copies(r):
            cp.wait()
        return carry

    lax.fori_loop(0, n_rows, start, 0, unroll=ROW_DMA_UNROLL)
    lax.fori_loop(0, n_rows, wait, 0, unroll=ROW_DMA_UNROLL)


def _dispatch_kernel(pos_ref, h_ref, xs_in_ref, xs_ref, sem):
    del xs_in_ref
    tm = h_ref.shape[0]

    def copies(r):
        return [_row_copy(h_ref, r, xs_ref, pos_ref[0, 0, k * tm + r], sem) for k in range(2)]

    _start_and_wait_rows(copies, tm)


def _dispatch_call(pos, h2, n_slots):
    t, d = h2.shape
    tm = pos.shape[2] // 2
    return pl.pallas_call(
        _dispatch_kernel,
        grid=(t // tm,),
        in_specs=[pl.BlockSpec((1, 1, 2 * tm), lambda i: (i, 0, 0), memory_space=pltpu.SMEM),
                  pl.BlockSpec((tm, d), lambda i: (i, 0)),
                  pl.BlockSpec(memory_space=pltpu.HBM)],
        out_specs=pl.BlockSpec(memory_space=pltpu.HBM),
        out_shape=jax.ShapeDtypeStruct((n_slots, d), F32),
        scratch_shapes=[pltpu.SemaphoreType.DMA(())],
        input_output_aliases={2: 0},
        compiler_params=_params("arbitrary"),
        name="dispatch",
    )(pos, h2, jnp.zeros((n_slots, d), F32))


def _experts_kernel(te_ref, nu_ref, xs_ref, w1_ref, w3_ref, w2_ref, ys_ref, w1_s, w3_s, w2_s):
    i = pl.program_id(0)

    @pl.when(i < nu_ref[0])
    def _():
        @pl.when((i == 0) | (te_ref[i] != te_ref[jnp.maximum(i - 1, 0)]))
        def _():
            w1_s[...] = w1_ref[0].astype(BF16)
            w3_s[...] = w3_ref[0].astype(BF16)
            w2_s[...] = w2_ref[0].astype(BF16)

        xb = xs_ref[...].astype(BF16)
        up = _silu(_bdot(xb, w1_s[...])) * _bdot(xb, w3_s[...])
        ys_ref[...] = _bdot(up, w2_s[...])

    @pl.when(i >= nu_ref[0])
    def _():
        ys_ref[...] = jnp.zeros_like(ys_ref)


def _experts_call(tile_expert, n_used, xs, w1, w3, w2):
    n_slots, d = xs.shape
    _, _, d_e = w1.shape
    n_tiles = n_slots // MOE_TILE
    tok = pl.BlockSpec((MOE_TILE, d), lambda i, te, nu: (jnp.minimum(i, nu[0] - 1), 0))
    return pl.pallas_call(
        _experts_kernel,
        grid_spec=pltpu.PrefetchScalarGridSpec(
            num_scalar_prefetch=2,
            grid=(n_tiles,),
            in_specs=[tok,
                      pl.BlockSpec((1, d, d_e), lambda i, te, nu: (te[i], 0, 0)),
                      pl.BlockSpec((1, d, d_e), lambda i, te, nu: (te[i], 0, 0)),
                      pl.BlockSpec((1, d_e, d), lambda i, te, nu: (te[i], 0, 0))],
            out_specs=pl.BlockSpec((MOE_TILE, d), lambda i, te, nu: (i, 0)),
            scratch_shapes=[pltpu.VMEM((d, d_e), BF16), pltpu.VMEM((d, d_e), BF16), pltpu.VMEM((d_e, d), BF16)]),
        out_shape=jax.ShapeDtypeStruct((n_slots, d), F32),
        compiler_params=_params("arbitrary"),
        name="experts",
    )(tile_expert, n_used, xs, w1, w3, w2)


def _combine_kernel(pos_ref, route_ref, x1_ref, gate_ref, fg_ref, ys_ref, o_ref, y1_s, y2_s, sem):
    tm = x1_ref.shape[0]

    def copies(r):
        return [_row_copy(ys_ref, pos_ref[0, 0, r], y1_s, r, sem),
                _row_copy(ys_ref, pos_ref[0, 0, tm + r], y2_s, r, sem)]

    _start_and_wait_rows(copies, tm)
    route = route_ref[...]
    y = route[:, 2:3] * y1_s[...] + route[:, 3:4] * y2_s[...]
    x2 = x1_ref[...] + gate_ref[0] * y
    o_ref[...] = x2 * lax.rsqrt(jnp.mean(x2 * x2, axis=-1, keepdims=True) + EPS) * fg_ref[...]


def _combine_call(pos, route, x1, gate, final_g, ys, seq):
    t, d = x1.shape
    tm = pos.shape[2] // 2
    per_b = seq // tm
    tok = pl.BlockSpec((tm, d), lambda i: (i, 0))
    return pl.pallas_call(
        _combine_kernel,
        grid=(t // tm,),
        in_specs=[pl.BlockSpec((1, 1, 2 * tm), lambda i: (i, 0, 0), memory_space=pltpu.SMEM),
                  pl.BlockSpec((tm, LANES), lambda i: (i, 0)), tok,
                  pl.BlockSpec((1, 1, d), lambda i: (i // per_b, 0, 0)),
                  pl.BlockSpec((1, d), lambda i: (0, 0)),
                  pl.BlockSpec(memory_space=pltpu.HBM)],
        out_specs=tok,
        out_shape=jax.ShapeDtypeStruct((t, d), F32),
        scratch_shapes=[pltpu.VMEM((tm, d), F32), pltpu.VMEM((tm, d), F32), pltpu.SemaphoreType.DMA(())],
        compiler_params=_params("arbitrary"),
        name="combine",
    )(pos, route, x1, gate, final_g.reshape(1, d), ys)


def _moe_plan(route, cnt, tm):
    t = route.shape[0]
    counts = cnt[0, :N_EXPERTS].astype(jnp.int32)
    n_tiles_e = (counts + MOE_TILE - 1) // MOE_TILE
    tile_end = jnp.cumsum(n_tiles_e)
    offsets = (tile_end - n_tiles_e) * MOE_TILE
    n_used = tile_end[-1]
    experts = jnp.arange(N_EXPERTS, dtype=jnp.int32)

    def slot(eid, rank):
        base = jnp.sum(jnp.where(eid.astype(jnp.int32)[:, None] == experts[None, :], offsets[None, :], 0), axis=1)
        return base + rank.astype(jnp.int32)

    pos1 = slot(route[:, 0], route[:, 4]).reshape(t // tm, tm)
    pos2 = slot(route[:, 1], route[:, 5]).reshape(t // tm, tm)
    pos = jnp.concatenate([pos1, pos2], axis=1)[:, None, :]
    n_tiles = (2 * t) // MOE_TILE + N_EXPERTS
    tiles = jnp.arange(n_tiles, dtype=jnp.int32)
    tile_expert = jnp.sum(jnp.minimum(tiles, n_used - 1)[:, None] >= tile_end[None, :], axis=1).astype(jnp.int32)
    return pos, tile_expert, n_used.reshape(1).astype(jnp.int32), n_tiles * MOE_TILE


def kernel(x, c, ctx, c_ctx, w_mod, b_mod, norm1_g, w_in, conv_w, conv_b, lru_wa, lru_ba, lru_wx, lru_bx,
           lru_lam, hg_lb, hg_norm_g, w_proj_a, w_proj_b, w_out, norm2_g, w_rg, b_rg, w_re, b_re,
           w1, w3, w2, final_g):
    bsz, seq, d = x.shape
    seq_c = ctx.shape[1]
    l = 0

    pad = (-(bsz + 1)) % SUBLANES
    cc = jnp.concatenate([c, c_ctx[None, :], jnp.zeros((pad, d), F32)], axis=0)
    mod = _mod_call(cc, w_mod[l], b_mod[l])
    mx = [mod[:bsz, None, k * d:(k + 1) * d] for k in range(6)]
    mc = [jnp.broadcast_to(mod[bsz:bsz + 1, None, k * d:(k + 1) * d], (bsz, 1, d)) for k in range(2)]
    lb = jax.nn.softmax(hg_lb.astype(F32), axis=0)[l]

    hx = _norm_call(x, norm1_g[l], mx[0], mx[1])
    hc = _norm_call(ctx, norm1_g[l], mc[0], mc[1]).reshape(bsz * seq_c, d)
    rows = seq // GRID_W
    hx_cm = hx.reshape(bsz, rows, GRID_W, d).swapaxes(1, 2).reshape(bsz * seq, d)
    hx = hx.reshape(bsz * seq, d)
    z = _proj_call(hx, w_in[l], lambda j: jnp.where(j < 2, j, 7), 3, "proj_r")
    zh = _proj_call(hx_cm, w_in[l], lambda j: jnp.where(j < 5, j + 2, 8), 6, "proj_h")
    zc = _proj_call(hc, w_in[l], lambda j: jnp.where(j > 0, j + 2, 0), 4, "proj_c")

    ya = _lru_call(z, zc, conv_w[l], conv_b[l], lru_wa[l], lru_wx[l], lru_ba[l], lru_bx[l], lru_lam[l], bsz)
    s_f, s_b = _hgc_call(zc, lb, bsz)
    o_f = _hg_call(zh, 1, lb, s_f, bsz, reverse=False)
    mb_cm = _hg_call(zh, 2, lb, s_b, bsz, reverse=True, post=(o_f, hg_norm_g[l], w_proj_b[l]))
    mb = mb_cm.reshape(bsz, GRID_W, rows, d).swapaxes(1, 2).reshape(bsz, seq, d)

    w_router = jnp.zeros((d, LANES), F32).at[:, :N_EXPERTS].set(w_re[l])
    w_router = w_router.at[:, N_EXPERTS:N_EXPERTS + N_GROUPS].set(w_rg[l])
    b_router = jnp.zeros((1, LANES), F32).at[0, :N_EXPERTS].set(b_re[l])
    b_router = b_router.at[0, N_EXPERTS:N_EXPERTS + N_GROUPS].set(b_rg[l])
    x1, h2, route, cnt = _merge_call(x, ya, z, mb, mx[2], mx[3], mx[4], norm2_g[l],
                                     w_proj_a[l], w_out[l], w_router, b_router)

    route = route.reshape(bsz * seq, LANES)
    pos, tile_expert, n_used, n_slots = _moe_plan(route, cnt, min(seq, 512))
    xs = _dispatch_call(pos, h2.reshape(bsz * seq, d), n_slots)
    ys = _experts_call(tile_expert, n_used, xs, w1[l], w3[l], w2[l])
    out = _combine_call(pos, route, x1.reshape(bsz * seq, d), mx[5], final_g, ys, seq)
    return out.reshape(bsz, seq, d)
```

```python
import functools

import jax
import jax.numpy as jnp
from jax import lax
from jax.experimental import pallas as pl
from jax.experimental.pallas import tpu as pltpu

F32 = jnp.float32
BF16 = jnp.bfloat16
HIGHEST = lax.Precision.HIGHEST

EPS = 1e-6
GRID_W = 64
CHUNK = 32
LRU_C = 8.0
CONV_W = 4
N_HEADS = 8
HEAD_DIM = 128
N_GROUPS = 4
EXP_PER_GROUP = 8
N_EXPERTS = N_GROUPS * EXP_PER_GROUP
LANES = 128
SUBLANES = 8
VMEM_LIMIT = 56 * 1024 * 1024


def _params(*sem):
    return pltpu.CompilerParams(dimension_semantics=sem, vmem_limit_bytes=VMEM_LIMIT)


def _sigmoid(x):
    return 0.5 * jnp.tanh(0.5 * x) + 0.5


def _silu(x):
    return x * _sigmoid(x)


def _bdot(a, b):
    return jnp.dot(a.astype(BF16), b.astype(BF16), preferred_element_type=F32)


def _mod_kernel(c_ref, w_ref, b_ref, o_ref):
    o_ref[...] = jnp.dot(_silu(c_ref[...]), w_ref[...], precision=HIGHEST,
                         preferred_element_type=F32) + b_ref[...]


def _mod_call(cc, w_mod, b_mod):
    rows, d = cc.shape
    n = w_mod.shape[1]
    tn = 1024
    return pl.pallas_call(
        _mod_kernel,
        grid=(n // tn,),
        in_specs=[pl.BlockSpec((rows, d), lambda j: (0, 0)),
                  pl.BlockSpec((d, tn), lambda j: (0, j)),
                  pl.BlockSpec((1, tn), lambda j: (0, j))],
        out_specs=pl.BlockSpec((rows, tn), lambda j: (0, j)),
        out_shape=jax.ShapeDtypeStruct((rows, n), F32),
        compiler_params=_params("arbitrary"),
        name="mod",
    )(cc, w_mod, b_mod.reshape(1, n))


def _norm_mod(x, g, shift, scale):
    y = x * lax.rsqrt(jnp.mean(x * x, axis=-1, keepdims=True) + EPS) * g
    return y * (1.0 + scale) + shift


def _norm_kernel(x_ref, g_ref, shift_ref, scale_ref, o_ref):
    o_ref[0] = _norm_mod(x_ref[0], g_ref[...], shift_ref[0], scale_ref[0]).astype(o_ref.dtype)


def _norm_call(x, g, shift, scale):
    b, l, d = x.shape
    tl = min(l, 512)
    vec = pl.BlockSpec((1, 1, d), lambda i, j: (i, 0, 0))
    return pl.pallas_call(
        _norm_kernel,
        grid=(b, l // tl),
        in_specs=[pl.BlockSpec((1, tl, d), lambda i, j: (i, j, 0)),
                  pl.BlockSpec((1, d), lambda i, j: (0, 0)), vec, vec],
        out_specs=pl.BlockSpec((1, tl, d), lambda i, j: (i, j, 0)),
        out_shape=jax.ShapeDtypeStruct((b, l, d), BF16),
        compiler_params=_params("arbitrary", "arbitrary"),
        name="norm",
    )(x, g.reshape(1, d), shift, scale)


def _proj_kernel(a_ref, w_ref, o_ref):
    o_ref[0] = jnp.dot(a_ref[...], w_ref[...].astype(BF16), preferred_element_type=F32)


def _proj_call(a, w, col_of, n_out, name):
    t, k = a.shape
    tn = 1024
    tm = min(t, 2048)
    return pl.pallas_call(
        _proj_kernel,
        grid=(t // tm, n_out),
        in_specs=[pl.BlockSpec((tm, k), lambda i, j: (i, 0)),
                  pl.BlockSpec((k, tn), lambda i, j: (0, col_of(j)))],
        out_specs=pl.BlockSpec((1, tm, tn), lambda i, j: (j, i, 0)),
        out_shape=jax.ShapeDtypeStruct((n_out, t, tn), F32),
        compiler_params=_params("arbitrary", "arbitrary"),
        name=name,
    )(a, w)


def _shift_rows(x, k):
    n = x.shape[0]
    row = lax.broadcasted_iota(jnp.int32, x.shape, 0)
    rolled = pltpu.roll(x, k % n, 0)
    keep = (row >= k) if k > 0 else (row < n + k)
    return jnp.where(keep, rolled, 0.0)


def _lru_kernel(ax_ref, axc_ref, cw_ref, cb_ref, wa_ref, wx_ref, ba_ref, bx_ref, lam_ref,
                o_ref, a_s, b_s):
    seq, width = ax_ref.shape[1], ax_ref.shape[2]
    seq_c = axc_ref.shape[1]
    heads = width // HEAD_DIM

    def conv(x):
        acc = _shift_rows(x, 2) * cw_ref[0:1, :]
        acc = acc + _shift_rows(x, 1) * cw_ref[1:2, :]
        acc = acc + x * cw_ref[2:3, :]
        acc = acc + _shift_rows(x, -1) * cw_ref[3:4, :]
        return acc + cb_ref[...]

    def coeffs(u, d, h):
        hs = slice(h * HEAD_DIM, (h + 1) * HEAD_DIM)
        ub = u.astype(BF16)
        r = _sigmoid(_bdot(ub, wa_ref[d, h]) + ba_ref[d:d + 1, hs])
        i = _sigmoid(_bdot(ub, wx_ref[d, h]) + bx_ref[d:d + 1, hs])
        log_a = r * (-LRU_C * jax.nn.softplus(-lam_ref[d:d + 1, hs]))
        a = jnp.exp(log_a)
        return a, jnp.sqrt(1.0 - a * a) * (i * u)

    u_c = conv(axc_ref[0])
    u_x = conv(ax_ref[0])
    n_blk = seq // SUBLANES
    row8 = lax.broadcasted_iota(jnp.int32, (SUBLANES, width), 0)

    for d, reverse in enumerate((False, True)):
        edge = u_c[seq_c - SUBLANES:] if reverse else u_c[:SUBLANES]
        pick = SUBLANES - 1 if reverse else 0
        h0 = jnp.concatenate([coeffs(edge[:, h * HEAD_DIM:(h + 1) * HEAD_DIM], d, h)[1][pick:pick + 1]
                              for h in range(heads)], axis=1)
        for h in range(heads):
            hs = slice(h * HEAD_DIM, (h + 1) * HEAD_DIM)
            a, b = coeffs(u_x[:, hs], d, h)
            a_s[:, hs] = a
            b_s[:, hs] = b

        def body(i, h_prev, reverse=reverse):
            blk = (n_blk - 1 - i) if reverse else i
            rows = pl.ds(pl.multiple_of(blk * SUBLANES, SUBLANES), SUBLANES)
            a = a_s[rows, :]
            b = b_s[rows, :]
            for s in (1, 2, 4):
                if reverse:
                    keep = row8 < SUBLANES - s
                    sh = SUBLANES - s
                else:
                    keep = row8 >= s
                    sh = s
                a_sh = jnp.where(keep, pltpu.roll(a, sh, 0), 1.0)
                b_sh = jnp.where(keep, pltpu.roll(b, sh, 0), 0.0)
                b = a * b_sh + b
                a = a * a_sh
            hh = b + a * h_prev
            if reverse:
                o_ref[0, rows, :] = o_ref[0, rows, :] + hh
                return hh[0:1, :]
            o_ref[0, rows, :] = hh
            return hh[SUBLANES - 1:SUBLANES, :]

        lax.fori_loop(0, n_blk, body, h0, unroll=4)


def _lru_call(z, zc, conv_w, conv_b, wa, wx, ba, bx, lam, bsz):
    t = z.shape[1]
    seq = t // bsz
    seq_c = zc.shape[1] // bsz
    width = 2 * HEAD_DIM
    n_w = (N_HEADS * HEAD_DIM) // width
    z4 = z.reshape(z.shape[0], bsz, seq, z.shape[2])
    zc4 = zc.reshape(zc.shape[0], bsz, seq_c, zc.shape[2])
    vec2 = pl.BlockSpec((2, width), lambda b, w: (0, w))
    wspec = pl.BlockSpec((2, width // HEAD_DIM, HEAD_DIM, HEAD_DIM), lambda b, w: (0, w, 0, 0))

    def kern(ax_ref, axc_ref, *rest):
        _lru_kernel(ax_ref.at[0], axc_ref.at[0], *rest)

    return pl.pallas_call(
        kern,
        grid=(bsz, n_w),
        in_specs=[pl.BlockSpec((1, 1, seq, width), lambda b, w: (0, b, 0, w)),
                  pl.BlockSpec((1, 1, seq_c, width), lambda b, w: (0, b, 0, w)),
                  pl.BlockSpec((CONV_W, width), lambda b, w: (0, w)),
                  pl.BlockSpec((1, width), lambda b, w: (0, w)),
                  wspec, wspec, vec2, vec2, vec2],
        out_specs=pl.BlockSpec((1, seq, width), lambda b, w: (b, 0, w)),
        out_shape=jax.ShapeDtypeStruct((bsz, seq, N_HEADS * HEAD_DIM), F32),
        scratch_shapes=[pltpu.VMEM((seq, width), F32), pltpu.VMEM((seq, width), F32)],
        compiler_params=_params("arbitrary", "arbitrary"),
        name="lru",
    )(z4, zc4, conv_w, conv_b.reshape(1, -1), wa, wx, ba, bx, lam)


def _split3(x):
    x1 = x.astype(BF16)
    r1 = x - x1.astype(F32)
    x2 = r1.astype(BF16)
    x3 = (r1 - x2.astype(F32)).astype(BF16)
    return x1, x2, x3


def _tri_cumsum(tri, x):
    x1, x2, x3 = _split3(x)
    dot = functools.partial(jnp.dot, preferred_element_type=F32)
    return dot(tri, x1) + dot(tri, x2) + dot(tri, x3)


def _hg_gates(f, lb):
    g = lb + (1.0 - lb) * _sigmoid(f)
    return jnp.log(g), 1.0 - g


def _dot_tn(a, b):
    return lax.dot_general(a.astype(BF16), b.astype(BF16), (((0,), (0,)), ((), ())),
                           preferred_element_type=F32)


def _dot_nt(a, b):
    return lax.dot_general(a.astype(BF16), b.astype(BF16), (((1,), (1,)), ((), ())),
                           preferred_element_type=F32)


def _hgc_kernel(ff_ref, fb_ref, v_ref, lb_ref, sf_ref, sb_ref):
    n = ff_ref.shape[1]
    row = lax.broadcasted_iota(jnp.int32, (n, n), 0)
    col = lax.broadcasted_iota(jnp.int32, (n, n), 1)
    tri = (col <= row).astype(BF16)
    lb = lb_ref[...]
    v = v_ref[0]
    lg, k = _hg_gates(ff_ref[0], lb)
    lc = _tri_cumsum(tri, lg)
    kf = k * jnp.exp(lc[n - 1:n, :] - lc)
    lg, k = _hg_gates(fb_ref[0], lb)
    lc = _tri_cumsum(tri, lg)
    kb = k * jnp.exp(lc - lg)
    for h in range(N_HEADS):
        hs = slice(h * HEAD_DIM, (h + 1) * HEAD_DIM)
        sf_ref[0, h] = _dot_tn(v[:, hs], kf[:, hs])
        sb_ref[0, h] = _dot_tn(v[:, hs], kb[:, hs])


def _hgc_call(zc, lb, bsz):
    seq_c = zc.shape[1] // bsz
    width = zc.shape[2]
    zc4 = zc.reshape(zc.shape[0], bsz, seq_c, width)

    def slab(j):
        return pl.BlockSpec((1, 1, seq_c, width), lambda b: (j, b, 0, 0))

    def kern(ff_ref, fb_ref, v_ref, lb_ref, sf_ref, sb_ref):
        _hgc_kernel(ff_ref.at[0], fb_ref.at[0], v_ref.at[0], lb_ref, sf_ref, sb_ref)

    sspec = pl.BlockSpec((1, N_HEADS, HEAD_DIM, HEAD_DIM), lambda b: (b, 0, 0, 0))
    sshape = jax.ShapeDtypeStruct((bsz, N_HEADS, HEAD_DIM, HEAD_DIM), F32)
    return pl.pallas_call(
        kern,
        grid=(bsz,),
        in_specs=[slab(1), slab(2), slab(3), pl.BlockSpec((1, width), lambda b: (0, 0))],
        out_specs=[sspec, sspec],
        out_shape=[sshape, sshape],
        compiler_params=_params("arbitrary"),
        name="hgc",
    )(zc4, zc4, zc4, lb.reshape(1, width))


def _hg_kernel(q_ref, f_ref, v_ref, lb_ref, o_ref, st_s, *, reverse, n_chunks, post):
    width = N_HEADS * HEAD_DIM
    rows = n_chunks * CHUNK

    row = lax.broadcasted_iota(jnp.int32, (rows, rows), 0)
    col = lax.broadcasted_iota(jnp.int32, (rows, rows), 1)
    same = (row // CHUNK) == (col // CHUNK)
    causal = same & ((col >= row) if reverse else (col <= row))
    last, mid = (0, CHUNK - 1 - CHUNK // 2) if reverse else (CHUNK - 1, CHUNK // 2)

    def chunked(t):
        return t.reshape(n_chunks, CHUNK, width)

    def flat(t):
        return t.reshape(rows, width)

    def chunk_cumsum(x):
        pos = lax.broadcasted_iota(jnp.int32, x.shape, 0) % CHUNK
        s = 1
        while s < CHUNK:
            if reverse:
                x = x + jnp.where(pos < CHUNK - s, pltpu.roll(x, rows - s, 0), 0.0)
            else:
                x = x + jnp.where(pos >= s, pltpu.roll(x, s, 0), 0.0)
            s *= 2
        return x

    q = chunked(_silu(q_ref[0]))
    v = v_ref[0]
    lg, k = _hg_gates(f_ref[0], lb_ref[...])
    k = chunked(k)
    lc = chunked(chunk_cumsum(lg))
    ltot = lc[:, last:last + 1, :]
    lmid = lc[:, mid:mid + 1, :]
    e_tot = jnp.exp(ltot)
    q_at = q * jnp.exp(lc - lmid)
    k_at = k * jnp.exp(lmid - lc)
    q_in = flat(q_at * jnp.exp(lmid))
    k_st = flat(k_at * jnp.exp(ltot - lmid))
    q_at = flat(q_at)
    k_at = flat(k_at)

    heads = [slice(h * HEAD_DIM, (h + 1) * HEAD_DIM) for h in range(N_HEADS)]
    chunks = [slice(c * CHUNK, (c + 1) * CHUNK) for c in range(n_chunks)]
    order = list(reversed(range(n_chunks))) if reverse else list(range(n_chunks))
    att = [jnp.where(causal, _dot_nt(q_at[:, hs], k_at[:, hs]), 0.0) for hs in heads]
    o_intra = [_bdot(att[h], v[:, hs]) for h, hs in enumerate(heads)]
    d_st = [[_dot_tn(v[rs, hs], k_st[rs, hs]) for rs in chunks] for hs in heads]
    outs = []
    for h, hs in enumerate(heads):
        st = st_s[h]
        st_in = [None] * n_chunks
        for c in order:
            st_in[c] = st
            st = st * e_tot[c, :, hs] + d_st[h][c]
        st_s[h] = st
        o_inter = [_dot_nt(q_in[rs, hs], st_in[c]) for c, rs in enumerate(chunks)]
        outs.append(o_intra[h] + jnp.concatenate(o_inter, axis=0))

    if post is None:
        for h in range(N_HEADS):
            o_ref[0, :, h * HEAD_DIM:(h + 1) * HEAD_DIM] = outs[h]
        return
    of_ref, og_ref, gb_ref, hgn_ref, wpb_ref = post
    parts = []
    for h in range(N_HEADS):
        hs = slice(h * HEAD_DIM, (h + 1) * HEAD_DIM)
        o = of_ref[0, :, hs] + outs[h]
        parts.append(o * lax.rsqrt(jnp.mean(o * o, axis=-1, keepdims=True) + EPS) * hgn_ref[:, hs])
    yb = jnp.concatenate(parts, axis=1)
    o_ref[0] = _sigmoid(gb_ref[0]) * _bdot(yb * _silu(og_ref[0]), wpb_ref[...])


def _hg_call(zh, f_slab, lb, s0, bsz, reverse, post=None):
    t, width = zh.shape[1], zh.shape[2]
    seq = t // bsz
    assert seq // GRID_W == CHUNK
    n_chunks = 8
    rows = n_chunks * CHUNK
    n_grp = seq // rows
    z4 = zh.reshape(zh.shape[0], bsz, seq, width)

    def grp(g):
        return (n_grp - 1 - g) if reverse else g

    nb = 2 if bsz % 2 == 0 else 1

    def slab(j):
        return pl.BlockSpec((1, nb, rows, width), lambda b, g: (j, b, grp(g), 0))

    tok = pl.BlockSpec((nb, rows, width), lambda b, g: (b, grp(g), 0))
    row = pl.BlockSpec((1, width), lambda b, g: (0, 0))
    in_specs = [slab(0), slab(f_slab), slab(3), row,
                pl.BlockSpec((nb, N_HEADS, HEAD_DIM, HEAD_DIM), lambda b, g: (b, 0, 0, 0))]
    args = [z4, z4, z4, lb.reshape(1, width), s0]
    if post is not None:
        o_other, hg_norm_g, w_proj_b = post
        in_specs += [tok, slab(4), slab(5), row, pl.BlockSpec((width, width), lambda b, g: (0, 0))]
        args += [o_other, z4, z4, hg_norm_g.reshape(1, width), w_proj_b.astype(BF16)]

    def kern(q_ref, f_ref, v_ref, lb_ref, s0_ref, *rest):
        *extra, o_ref, st_s = rest

        @pl.when(pl.program_id(1) == 0)
        def _():
            st_s[...] = s0_ref[...]

        for bi in range(nb):
            one = pl.ds(bi, 1)
            post_refs = None
            if extra:
                of_ref, og_ref, gb_ref, hgn_ref, wpb_ref = extra
                post_refs = (of_ref.at[one], og_ref.at[0, one], gb_ref.at[0, one], hgn_ref, wpb_ref)
            _hg_kernel(q_ref.at[0, one], f_ref.at[0, one], v_ref.at[0, one], lb_ref, o_ref.at[one],
                       st_s.at[bi], reverse=reverse, n_chunks=n_chunks, post=post_refs)

    return pl.pallas_call(
        kern,
        grid=(bsz // nb, n_grp),
        in_specs=in_specs,
        out_specs=tok,
        out_shape=jax.ShapeDtypeStruct((bsz, seq, width), F32),
        scratch_shapes=[pltpu.VMEM((nb, N_HEADS, HEAD_DIM, HEAD_DIM), F32)],
        compiler_params=_params("arbitrary", "arbitrary"),
        name="hg_bwd" if reverse else "hg_fwd",
    )(*args)


def _merge_kernel(x_ref, ya_ref, ay_ref, ga_ref, mb_ref,
                  gate_ref, shift_ref, scale_ref, n2_ref,
                  wpa_ref, wo_ref, wr_ref, br_ref,
                  x1_ref, h2_ref, route_ref, cnt_ref, run_s):
    tm = x_ref.shape[1]

    @pl.when((pl.program_id(0) == 0) & (pl.program_id(1) == 0))
    def _():
        run_s[...] = jnp.zeros_like(run_s)

    pa = _bdot(ya_ref[0] * jax.nn.gelu(ay_ref[0, 0]), wpa_ref[...])
    merged = _sigmoid(ga_ref[0, 0]) * pa + mb_ref[0]
    x1 = x_ref[0] + gate_ref[0] * _bdot(merged, wo_ref[...])
    x1_ref[0] = x1
    h2 = _norm_mod(x1, n2_ref[...], shift_ref[0], scale_ref[0])
    h2_ref[0] = h2

    h_hi, h_lo, _ = _split3(h2)
    w_hi, w_lo, _ = _split3(wr_ref[...])
    dot = functools.partial(jnp.dot, preferred_element_type=F32)
    logit = dot(h_hi, w_hi) + (dot(h_hi, w_lo) + dot(h_lo, w_hi)) + br_ref[...]
    col = lax.broadcasted_iota(jnp.int32, (tm, LANES), 1)
    colf = col.astype(F32)
    neg = -jnp.inf

    def first_at(mask):
        return -jnp.max(jnp.where(mask, -colf, -1e9), axis=1, keepdims=True)

    is_g = (col >= N_EXPERTS) & (col < N_EXPERTS + N_GROUPS)
    lgm = jnp.where(is_g, logit, neg)
    mg = jnp.max(lgm, axis=1, keepdims=True)
    pg = 1.0 / jnp.sum(jnp.exp(lgm - mg), axis=1, keepdims=True)
    gi = first_at(is_g & (logit == mg)) - float(N_EXPERTS)
    is_e = (col < N_EXPERTS) & ((col // EXP_PER_GROUP).astype(F32) == gi)
    lem = jnp.where(is_e, logit, neg)
    ee = jnp.exp(lem - jnp.max(lem, axis=1, keepdims=True))
    pe = ee / jnp.sum(ee, axis=1, keepdims=True)
    p1 = jnp.max(jnp.where(is_e, pe, -1.0), axis=1, keepdims=True)
    i1 = first_at(is_e & (pe == p1))
    rest = is_e & (colf != i1)
    p2 = jnp.max(jnp.where(rest, pe, -1.0), axis=1, keepdims=True)
    i2 = first_at(rest & (pe == p2))
    den = p1 + p2

    r_i = lax.broadcasted_iota(jnp.int32, (tm, tm), 0)
    c_i = lax.broadcasted_iota(jnp.int32, (tm, tm), 1)
    earlier = (c_i < r_i).astype(BF16)
    oh1 = (colf == i1).astype(F32)
    oh2 = (colf == i2).astype(F32)
    run = run_s[...]
    rank1 = jnp.sum(oh1 * (_bdot(earlier, oh1) + run), axis=1, keepdims=True)
    run = run + jnp.sum(oh1, axis=0, keepdims=True)
    rank2 = jnp.sum(oh2 * (_bdot(earlier, oh2) + run), axis=1, keepdims=True)
    run = run + jnp.sum(oh2, axis=0, keepdims=True)
    run_s[...] = run
    cnt_ref[...] = run
    vals = (i1, i2, pg * p1 / den, pg * p2 / den, rank1, rank2)
    route_ref[0] = sum(jnp.where(col == k, v, 0.0) for k, v in enumerate(vals))


def _merge_call(x, ya, z, mb, gate, shift, scale, norm2_g, w_proj_a, w_out, w_router, b_router):
    bsz, seq, d = x.shape
    tm = min(seq, 512)
    z4 = z.reshape(z.shape[0], bsz, seq, d)
    tok = pl.BlockSpec((1, tm, d), lambda b, i: (b, i, 0))

    def slab(j):
        return pl.BlockSpec((1, 1, tm, d), lambda b, i: (j, b, i, 0))

    vec = pl.BlockSpec((1, 1, d), lambda b, i: (b, 0, 0))
    row = pl.BlockSpec((1, d), lambda b, i: (0, 0))
    wsp = pl.BlockSpec((d, d), lambda b, i: (0, 0))
    return pl.pallas_call(
        _merge_kernel,
        grid=(bsz, seq // tm),
        in_specs=[tok, tok, slab(1), slab(2), tok,
                  vec, vec, vec, row, wsp, wsp,
                  pl.BlockSpec((d, LANES), lambda b, i: (0, 0)),
                  pl.BlockSpec((1, LANES), lambda b, i: (0, 0))],
        out_specs=[tok, tok, pl.BlockSpec((1, tm, LANES), lambda b, i: (b, i, 0)),
                   pl.BlockSpec((1, LANES), lambda b, i: (0, 0))],
        out_shape=[jax.ShapeDtypeStruct((bsz, seq, d), F32),
                   jax.ShapeDtypeStruct((bsz, seq, d), F32),
                   jax.ShapeDtypeStruct((bsz, seq, LANES), F32),
                   jax.ShapeDtypeStruct((1, LANES), F32)],
        scratch_shapes=[pltpu.VMEM((1, LANES), F32)],
        compiler_params=_params("arbitrary", "arbitrary"),
        name="merge",
    )(x, ya, z4, z4, mb, gate, shift, scale, norm2_g.reshape(1, d),
      w_proj_a.astype(BF16), w_out.astype(BF16), w_router, b_router)


MOE_TILE = 256


def _row_copy(src_ref, src_row, dst_ref, dst_row, sem):
    return pltpu.make_async_copy(src_ref.at[pl.ds(src_row, 1), :], dst_ref.at[pl.ds(dst_row, 1), :], sem)


ROW_DMA_UNROLL = 8


def _start_and_wait_rows(copies, n_rows):
    def start(r, carry):
        for k, cp in enumerate(copies(r)):
            cp.start(priority=k)
        return carry

    def wait(r, carry):
        for cp in copies(r):
            cp.wait()
        return carry

    lax.fori_loop(0, n_rows, start, 0, unroll=ROW_DMA_UNROLL)
    lax.fori_loop(0, n_rows, wait, 0, unroll=ROW_DMA_UNROLL)


def _dispatch_kernel(pos_ref, h_ref, xs_in_ref, xs_ref, sem):
    del xs_in_ref
    tm = h_ref.shape[0]

    def copies(r):
        return [_row_copy(h_ref, r, xs_ref, pos_ref[0, 0, k * tm + r], sem) for k in range(2)]

    _start_and_wait_rows(copies, tm)


def _dispatch_call(pos, h2, n_slots):
    t, d = h2.shape
    tm = pos.shape[2] // 2
    return pl.pallas_call(
        _dispatch_kernel,
        grid=(t // tm,),
        in_specs=[pl.BlockSpec((1, 1, 2 * tm), lambda i: (i, 0, 0), memory_space=pltpu.SMEM),
                  pl.BlockSpec((tm, d), lambda i: (i, 0)),
                  pl.BlockSpec(memory_space=pltpu.HBM)],
        out_specs=pl.BlockSpec(memory_space=pltpu.HBM),
        out_shape=jax.ShapeDtypeStruct((n_slots, d), F32),
        scratch_shapes=[pltpu.SemaphoreType.DMA(())],
        input_output_aliases={2: 0},
        compiler_params=_params("arbitrary"),
        name="dispatch",
    )(pos, h2, jnp.zeros((n_slots, d), F32))


def _experts_kernel(te_ref, nu_ref, xs_ref, w1_ref, w3_ref, w2_ref, ys_ref, w1_s, w3_s, w2_s):
    i = pl.program_id(0)

    @pl.when(i < nu_ref[0])
    def _():
        @pl.when((i == 0) | (te_ref[i] != te_ref[jnp.maximum(i - 1, 0)]))
        def _():
            w1_s[...] = w1_ref[0].astype(BF16)
            w3_s[...] = w3_ref[0].astype(BF16)
            w2_s[...] = w2_ref[0].astype(BF16)

        xb = xs_ref[...].astype(BF16)
        up = _silu(_bdot(xb, w1_s[...])) * _bdot(xb, w3_s[...])
        ys_ref[...] = _bdot(up, w2_s[...])

    @pl.when(i >= nu_ref[0])
    def _():
        ys_ref[...] = jnp.zeros_like(ys_ref)


def _experts_call(tile_expert, n_used, xs, w1, w3, w2):
    n_slots, d = xs.shape
    _, _, d_e = w1.shape
    n_tiles = n_slots // MOE_TILE
    tok = pl.BlockSpec((MOE_TILE, d), lambda i, te, nu: (jnp.minimum(i, nu[0] - 1), 0))
    return pl.pallas_call(
        _experts_kernel,
        grid_spec=pltpu.PrefetchScalarGridSpec(
            num_scalar_prefetch=2,
            grid=(n_tiles,),
            in_specs=[tok,
                      pl.BlockSpec((1, d, d_e), lambda i, te, nu: (te[i], 0, 0)),
                      pl.BlockSpec((1, d, d_e), lambda i, te, nu: (te[i], 0, 0)),
                      pl.BlockSpec((1, d_e, d), lambda i, te, nu: (te[i], 0, 0))],
            out_specs=pl.BlockSpec((MOE_TILE, d), lambda i, te, nu: (i, 0)),
            scratch_shapes=[pltpu.VMEM((d, d_e), BF16), pltpu.VMEM((d, d_e), BF16), pltpu.VMEM((d_e, d), BF16)]),
        out_shape=jax.ShapeDtypeStruct((n_slots, d), F32),
        compiler_params=_params("arbitrary"),
        name="experts",
    )(tile_expert, n_used, xs, w1, w3, w2)


def _combine_kernel(pos_ref, route_ref, x1_ref, gate_ref, fg_ref, ys_ref, o_ref, y1_s, y2_s, sem):
    tm = x1_ref.shape[0]

    def copies(r):
        return [_row_copy(ys_ref, pos_ref[0, 0, r], y1_s, r, sem),
                _row_copy(ys_ref, pos_ref[0, 0, tm + r], y2_s, r, sem)]

    _start_and_wait_rows(copies, tm)
    route = route_ref[...]
    y = route[:, 2:3] * y1_s[...] + route[:, 3:4] * y2_s[...]
    x2 = x1_ref[...] + gate_ref[0] * y
    o_ref[...] = x2 * lax.rsqrt(jnp.mean(x2 * x2, axis=-1, keepdims=True) + EPS) * fg_ref[...]


def _combine_call(pos, route, x1, gate, final_g, ys, seq):
    t, d = x1.shape
    tm = pos.shape[2] // 2
    per_b = seq // tm
    tok = pl.BlockSpec((tm, d), lambda i: (i, 0))
    return pl.pallas_call(
        _combine_kernel,
        grid=(t // tm,),
        in_specs=[pl.BlockSpec((1, 1, 2 * tm), lambda i: (i, 0, 0), memory_space=pltpu.SMEM),
                  pl.BlockSpec((tm, LANES), lambda i: (i, 0)), tok,
                  pl.BlockSpec((1, 1, d), lambda i: (i // per_b, 0, 0)),
                  pl.BlockSpec((1, d), lambda i: (0, 0)),
                  pl.BlockSpec(memory_space=pltpu.HBM)],
        out_specs=tok,
        out_shape=jax.ShapeDtypeStruct((t, d), F32),
        scratch_shapes=[pltpu.VMEM((tm, d), F32), pltpu.VMEM((tm, d), F32), pltpu.SemaphoreType.DMA(())],
        compiler_params=_params("arbitrary"),
        name="combine",
    )(pos, route, x1, gate, final_g.reshape(1, d), ys)


def _moe_plan(route, cnt, tm):
    t = route.shape[0]
    counts = cnt[0, :N_EXPERTS].astype(jnp.int32)
    n_tiles_e = (counts + MOE_TILE - 1) // MOE_TILE
    tile_end = jnp.cumsum(n_tiles_e)
    offsets = (tile_end - n_tiles_e) * MOE_TILE
    n_used = tile_end[-1]
    experts = jnp.arange(N_EXPERTS, dtype=jnp.int32)

    def slot(eid, rank):
        base = jnp.sum(jnp.where(eid.astype(jnp.int32)[:, None] == experts[None, :], offsets[None, :], 0), axis=1)
        return base + rank.astype(jnp.int32)

    pos1 = slot(route[:, 0], route[:, 4]).reshape(t // tm, tm)
    pos2 = slot(route[:, 1], route[:, 5]).reshape(t // tm, tm)
    pos = jnp.concatenate([pos1, pos2], axis=1)[:, None, :]
    n_tiles = (2 * t) // MOE_TILE + N_EXPERTS
    tiles = jnp.arange(n_tiles, dtype=jnp.int32)
    tile_expert = jnp.sum(jnp.minimum(tiles, n_used - 1)[:, None] >= tile_end[None, :], axis=1).astype(jnp.int32)
    return pos, tile_expert, n_used.reshape(1).astype(jnp.int32), n_tiles * MOE_TILE


def kernel(x, c, ctx, c_ctx, w_mod, b_mod, norm1_g, w_in, conv_w, conv_b, lru_wa, lru_ba, lru_wx, lru_bx,
           lru_lam, hg_lb, hg_norm_g, w_proj_a, w_proj_b, w_out, norm2_g, w_rg, b_rg, w_re, b_re,
           w1, w3, w2, final_g):
    bsz, seq, d = x.shape
    seq_c = ctx.shape[1]
    l = 0

    pad = (-(bsz + 1)) % SUBLANES
    cc = jnp.concatenate([c, c_ctx[None, :], jnp.zeros((pad, d), F32)], axis=0)
    mod = _mod_call(cc, w_mod[l], b_mod[l])
    mx = [mod[:bsz, None, k * d:(k + 1) * d] for k in range(6)]
    mc = [jnp.broadcast_to(mod[bsz:bsz + 1, None, k * d:(k + 1) * d], (bsz, 1, d)) for k in range(2)]
    lb = jax.nn.softmax(hg_lb.astype(F32), axis=0)[l]

    hx = _norm_call(x, norm1_g[l], mx[0], mx[1])
    hc = _norm_call(ctx, norm1_g[l], mc[0], mc[1]).reshape(bsz * seq_c, d)
    rows = seq // GRID_W
    hx_cm = hx.reshape(bsz, rows, GRID_W, d).swapaxes(1, 2).reshape(bsz * seq, d)
    hx = hx.reshape(bsz * seq, d)
    z = _proj_call(hx, w_in[l], lambda j: jnp.where(j < 2, j, 7), 3, "proj_r")
    zh = _proj_call(hx_cm, w_in[l], lambda j: jnp.where(j < 5, j + 2, 8), 6, "proj_h")
    zc = _proj_call(hc, w_in[l], lambda j: jnp.where(j > 0, j + 2, 0), 4, "proj_c")

    ya = _lru_call(z, zc, conv_w[l], conv_b[l], lru_wa[l], lru_wx[l], lru_ba[l], lru_bx[l], lru_lam[l], bsz)
    s_f, s_b = _hgc_call(zc, lb, bsz)
    o_f = _hg_call(zh, 1, lb, s_f, bsz, reverse=False)
    mb_cm = _hg_call(zh, 2, lb, s_b, bsz, reverse=True, post=(o_f, hg_norm_g[l], w_proj_b[l]))
    mb = mb_cm.reshape(bsz, GRID_W, rows, d).swapaxes(1, 2).reshape(bsz, seq, d)

    w_router = jnp.zeros((d, LANES), F32).at[:, :N_EXPERTS].set(w_re[l])
    w_router = w_router.at[:, N_EXPERTS:N_EXPERTS + N_GROUPS].set(w_rg[l])
    b_router = jnp.zeros((1, LANES), F32).at[0, :N_EXPERTS].set(b_re[l])
    b_router = b_router.at[0, N_EXPERTS:N_EXPERTS + N_GROUPS].set(b_rg[l])
    x1, h2, route, cnt = _merge_call(x, ya, z, mb, mx[2], mx[3], mx[4], norm2_g[l],
                                     w_proj_a[l], w_out[l], w_router, b_router)

    route = route.reshape(bsz * seq, LANES)
    pos, tile_expert, n_used, n_slots = _moe_plan(route, cnt, min(seq, 512))
    xs = _dispatch_call(pos, h2.reshape(bsz * seq, d), n_slots)
    ys = _experts_call(tile_expert, n_used, xs, w1[l], w3[l], w2[l])
    out = _combine_call(pos, route, x1.reshape(bsz * seq, d), mx[5], final_g, ys, seq)
    return out.reshape(bsz, seq, d)
```

```python
import functools

import jax
import jax.numpy as jnp
from jax import lax
from jax.experimental import pallas as pl
from jax.experimental.pallas import tpu as pltpu

F32 = jnp.float32
BF16 = jnp.bfloat16
HIGHEST = lax.Precision.HIGHEST

EPS = 1e-6
GRID_W = 64
CHUNK = 32
LRU_C = 8.0
CONV_W = 4
N_HEADS = 8
HEAD_DIM = 128
N_GROUPS = 4
EXP_PER_GROUP = 8
N_EXPERTS = N_GROUPS * EXP_PER_GROUP
LANES = 128
SUBLANES = 8
VMEM_LIMIT = 56 * 1024 * 1024


def _params(*sem):
    return pltpu.CompilerParams(dimension_semantics=sem, vmem_limit_bytes=VMEM_LIMIT)


def _sigmoid(x):
    return 0.5 * jnp.tanh(0.5 * x) + 0.5


def _silu(x):
    return x * _sigmoid(x)


def _bdot(a, b):
    return jnp.dot(a.astype(BF16), b.astype(BF16), preferred_element_type=F32)


def _mod_kernel(c_ref, w_ref, b_ref, o_ref):
    o_ref[...] = jnp.dot(_silu(c_ref[...]), w_ref[...], precision=HIGHEST,
                         preferred_element_type=F32) + b_ref[...]


def _mod_call(cc, w_mod, b_mod):
    rows, d = cc.shape
    n = w_mod.shape[1]
    tn = 1024
    return pl.pallas_call(
        _mod_kernel,
        grid=(n // tn,),
        in_specs=[pl.BlockSpec((rows, d), lambda j: (0, 0)),
                  pl.BlockSpec((d, tn), lambda j: (0, j)),
                  pl.BlockSpec((1, tn), lambda j: (0, j))],
        out_specs=pl.BlockSpec((rows, tn), lambda j: (0, j)),
        out_shape=jax.ShapeDtypeStruct((rows, n), F32),
        compiler_params=_params("arbitrary"),
        name="mod",
    )(cc, w_mod, b_mod.reshape(1, n))


def _norm_mod(x, g, shift, scale):
    y = x * lax.rsqrt(jnp.mean(x * x, axis=-1, keepdims=True) + EPS) * g
    return y * (1.0 + scale) + shift


def _norm_kernel(x_ref, g_ref, shift_ref, scale_ref, o_ref):
    o_ref[0] = _norm_mod(x_ref[0], g_ref[...], shift_ref[0], scale_ref[0]).astype(o_ref.dtype)


def _norm_call(x, g, shift, scale):
    b, l, d = x.shape
    tl = min(l, 512)
    vec = pl.BlockSpec((1, 1, d), lambda i, j: (i, 0, 0))
    return pl.pallas_call(
        _norm_kernel,
        grid=(b, l // tl),
        in_specs=[pl.BlockSpec((1, tl, d), lambda i, j: (i, j, 0)),
                  pl.BlockSpec((1, d), lambda i, j: (0, 0)), vec, vec],
        out_specs=pl.BlockSpec((1, tl, d), lambda i, j: (i, j, 0)),
        out_shape=jax.ShapeDtypeStruct((b, l, d), BF16),
        compiler_params=_params("arbitrary", "arbitrary"),
        name="norm",
    )(x, g.reshape(1, d), shift, scale)


def _proj_kernel(a_ref, w_ref, o_ref):
    o_ref[0] = jnp.dot(a_ref[...], w_ref[...].astype(BF16), preferred_element_type=F32)


def _proj_call(a, w, col_of, n_out, name):
    t, k = a.shape
    tn = 1024
    tm = min(t, 2048)
    return pl.pallas_call(
        _proj_kernel,
        grid=(t // tm, n_out),
        in_specs=[pl.BlockSpec((tm, k), lambda i, j: (i, 0)),
                  pl.BlockSpec((k, tn), lambda i, j: (0, col_of(j)))],
        out_specs=pl.BlockSpec((1, tm, tn), lambda i, j: (j, i, 0)),
        out_shape=jax.ShapeDtypeStruct((n_out, t, tn), F32),
        compiler_params=_params("arbitrary", "arbitrary"),
        name=name,
    )(a, w)


def _shift_rows(x, k):
    n = x.shape[0]
    row = lax.broadcasted_iota(jnp.int32, x.shape, 0)
    rolled = pltpu.roll(x, k % n, 0)
    keep = (row >= k) if k > 0 else (row < n + k)
    return jnp.where(keep, rolled, 0.0)


def _lru_kernel(ax_ref, axc_ref, cw_ref, cb_ref, wa_ref, wx_ref, ba_ref, bx_ref, lam_ref,
                o_ref, a_s, b_s):
    seq, width = ax_ref.shape[1], ax_ref.shape[2]
    seq_c = axc_ref.shape[1]
    heads = width // HEAD_DIM

    def conv(x):
        acc = _shift_rows(x, 2) * cw_ref[0:1, :]
        acc = acc + _shift_rows(x, 1) * cw_ref[1:2, :]
        acc = acc + x * cw_ref[2:3, :]
        acc = acc + _shift_rows(x, -1) * cw_ref[3:4, :]
        return acc + cb_ref[...]

    def coeffs(u, d, h):
        hs = slice(h * HEAD_DIM, (h + 1) * HEAD_DIM)
        ub = u.astype(BF16)
        r = _sigmoid(_bdot(ub, wa_ref[d, h]) + ba_ref[d:d + 1, hs])
        i = _sigmoid(_bdot(ub, wx_ref[d, h]) + bx_ref[d:d + 1, hs])
        log_a = r * (-LRU_C * jax.nn.softplus(-lam_ref[d:d + 1, hs]))
        a = jnp.exp(log_a)
        return a, jnp.sqrt(1.0 - a * a) * (i * u)

    u_c = conv(axc_ref[0])
    u_x = conv(ax_ref[0])
    n_blk = seq // SUBLANES
    row8 = lax.broadcasted_iota(jnp.int32, (SUBLANES, width), 0)

    for d, reverse in enumerate((False, True)):
        edge = u_c[seq_c - SUBLANES:] if reverse else u_c[:SUBLANES]
        pick = SUBLANES - 1 if reverse else 0
        h0 = jnp.concatenate([coeffs(edge[:, h * HEAD_DIM:(h + 1) * HEAD_DIM], d, h)[1][pick:pick + 1]
                              for h in range(heads)], axis=1)
        for h in range(heads):
            hs = slice(h * HEAD_DIM, (h + 1) * HEAD_DIM)
            a, b = coeffs(u_x[:, hs], d, h)
            a_s[:, hs] = a
            b_s[:, hs] = b

        def body(i, h_prev, reverse=reverse):
            blk = (n_blk - 1 - i) if reverse else i
            rows = pl.ds(pl.multiple_of(blk * SUBLANES, SUBLANES), SUBLANES)
            a = a_s[rows, :]
            b = b_s[rows, :]
            for s in (1, 2, 4):
                if reverse:
                    keep = row8 < SUBLANES - s
                    sh = SUBLANES - s
                else:
                    keep = row8 >= s
                    sh = s
                a_sh = jnp.where(keep, pltpu.roll(a, sh, 0), 1.0)
                b_sh = jnp.where(keep, pltpu.roll(b, sh, 0), 0.0)
                b = a * b_sh + b
                a = a * a_sh
            hh = b + a * h_prev
            if reverse:
                o_ref[0, rows, :] = o_ref[0, rows, :] + hh
                return hh[0:1, :]
            o_ref[0, rows, :] = hh
            return hh[SUBLANES - 1:SUBLANES, :]

        lax.fori_loop(0, n_blk, body, h0, unroll=4)


def _lru_call(z, zc, conv_w, conv_b, wa, wx, ba, bx, lam, bsz):
    t = z.shape[1]
    seq = t // bsz
    seq_c = zc.shape[1] // bsz
    width = 2 * HEAD_DIM
    n_w = (N_HEADS * HEAD_DIM) // width
    z4 = z.reshape(z.shape[0], bsz, seq, z.shape[2])
    zc4 = zc.reshape(zc.shape[0], bsz, seq_c, zc.shape[2])
    vec2 = pl.BlockSpec((2, width), lambda b, w: (0, w))
    wspec = pl.BlockSpec((2, width // HEAD_DIM, HEAD_DIM, HEAD_DIM), lambda b, w: (0, w, 0, 0))

    def kern(ax_ref, axc_ref, *rest):
        _lru_kernel(ax_ref.at[0], axc_ref.at[0], *rest)

    return pl.pallas_call(
        kern,
        grid=(bsz, n_w),
        in_specs=[pl.BlockSpec((1, 1, seq, width), lambda b, w: (0, b, 0, w)),
                  pl.BlockSpec((1, 1, seq_c, width), lambda b, w: (0, b, 0, w)),
                  pl.BlockSpec((CONV_W, width), lambda b, w: (0, w)),
                  pl.BlockSpec((1, width), lambda b, w: (0, w)),
                  wspec, wspec, vec2, vec2, vec2],
        out_specs=pl.BlockSpec((1, seq, width), lambda b, w: (b, 0, w)),
        out_shape=jax.ShapeDtypeStruct((bsz, seq, N_HEADS * HEAD_DIM), F32),
        scratch_shapes=[pltpu.VMEM((seq, width), F32), pltpu.VMEM((seq, width), F32)],
        compiler_params=_params("arbitrary", "arbitrary"),
        name="lru",
    )(z4, zc4, conv_w, conv_b.reshape(1, -1), wa, wx, ba, bx, lam)


def _split3(x):
    x1 = x.astype(BF16)
    r1 = x - x1.astype(F32)
    x2 = r1.astype(BF16)
    x3 = (r1 - x2.astype(F32)).astype(BF16)
    return x1, x2, x3


def _tri_cumsum(tri, x):
    x1, x2, x3 = _split3(x)
    dot = functools.partial(jnp.dot, preferred_element_type=F32)
    return dot(tri, x1) + dot(tri, x2) + dot(tri, x3)


def _hg_gates(f, lb):
    g = lb + (1.0 - lb) * _sigmoid(f)
    return jnp.log(g), 1.0 - g


def _dot_tn(a, b):
    return lax.dot_general(a.astype(BF16), b.astype(BF16), (((0,), (0,)), ((), ())),
                           preferred_element_type=F32)


def _dot_nt(a, b):
    return lax.dot_general(a.astype(BF16), b.astype(BF16), (((1,), (1,)), ((), ())),
                           preferred_element_type=F32)


def _hgc_kernel(ff_ref, fb_ref, v_ref, lb_ref, sf_ref, sb_ref):
    n = ff_ref.shape[1]
    row = lax.broadcasted_iota(jnp.int32, (n, n), 0)
    col = lax.broadcasted_iota(jnp.int32, (n, n), 1)
    tri = (col <= row).astype(BF16)
    lb = lb_ref[...]
    v = v_ref[0]
    lg, k = _hg_gates(ff_ref[0], lb)
    lc = _tri_cumsum(tri, lg)
    kf = k * jnp.exp(lc[n - 1:n, :] - lc)
    lg, k = _hg_gates(fb_ref[0], lb)
    lc = _tri_cumsum(tri, lg)
    kb = k * jnp.exp(lc - lg)
    for h in range(N_HEADS):
        hs = slice(h * HEAD_DIM, (h + 1) * HEAD_DIM)
        sf_ref[0, h] = _dot_tn(v[:, hs], kf[:, hs])
        sb_ref[0, h] = _dot_tn(v[:, hs], kb[:, hs])


def _hgc_call(zc, lb, bsz):
    seq_c = zc.shape[1] // bsz
    width = zc.shape[2]
    zc4 = zc.reshape(zc.shape[0], bsz, seq_c, width)

    def slab(j):
        return pl.BlockSpec((1, 1, seq_c, width), lambda b: (j, b, 0, 0))

    def kern(ff_ref, fb_ref, v_ref, lb_ref, sf_ref, sb_ref):
        _hgc_kernel(ff_ref.at[0], fb_ref.at[0], v_ref.at[0], lb_ref, sf_ref, sb_ref)

    sspec = pl.BlockSpec((1, N_HEADS, HEAD_DIM, HEAD_DIM), lambda b: (b, 0, 0, 0))
    sshape = jax.ShapeDtypeStruct((bsz, N_HEADS, HEAD_DIM, HEAD_DIM), F32)
    return pl.pallas_call(
        kern,
        grid=(bsz,),
        in_specs=[slab(1), slab(2), slab(3), pl.BlockSpec((1, width), lambda b: (0, 0))],
        out_specs=[sspec, sspec],
        out_shape=[sshape, sshape],
        compiler_params=_params("arbitrary"),
        name="hgc",
    )(zc4, zc4, zc4, lb.reshape(1, width))


def _hg_kernel(q_ref, f_ref, v_ref, lb_ref, o_ref, st_s, *, reverse, n_chunks, post):
    width = N_HEADS * HEAD_DIM
    rows = n_chunks * CHUNK

    row = lax.broadcasted_iota(jnp.int32, (rows, rows), 0)
    col = lax.broadcasted_iota(jnp.int32, (rows, rows), 1)
    same = (row // CHUNK) == (col // CHUNK)
    causal = same & ((col >= row) if reverse else (col <= row))
    last, mid = (0, CHUNK - 1 - CHUNK // 2) if reverse else (CHUNK - 1, CHUNK // 2)

    def chunked(t):
        return t.reshape(n_chunks, CHUNK, width)

    def flat(t):
        return t.reshape(rows, width)

    def chunk_cumsum(x):
        pos = lax.broadcasted_iota(jnp.int32, x.shape, 0) % CHUNK
        s = 1
        while s < CHUNK:
            if reverse:
                x = x + jnp.where(pos < CHUNK - s, pltpu.roll(x, rows - s, 0), 0.0)
            else:
                x = x + jnp.where(pos >= s, pltpu.roll(x, s, 0), 0.0)
            s *= 2
        return x

    q = chunked(_silu(q_ref[0]))
    v = v_ref[0]
    lg, k = _hg_gates(f_ref[0], lb_ref[...])
    k = chunked(k)
    lc = chunked(chunk_cumsum(lg))
    ltot = lc[:, last:last + 1, :]
    lmid = lc[:, mid:mid + 1, :]
    e_tot = jnp.exp(ltot)
    q_at = q * jnp.exp(lc - lmid)
    k_at = k * jnp.exp(lmid - lc)
    q_in = flat(q_at * jnp.exp(lmid))
    k_st = flat(k_at * jnp.exp(ltot - lmid))
    q_at = flat(q_at)
    k_at = flat(k_at)

    heads = [slice(h * HEAD_DIM, (h + 1) * HEAD_DIM) for h in range(N_HEADS)]
    chunks = [slice(c * CHUNK, (c + 1) * CHUNK) for c in range(n_chunks)]
    order = list(reversed(range(n_chunks))) if reverse else list(range(n_chunks))
    att = [jnp.where(causal, _dot_nt(q_at[:, hs], k_at[:, hs]), 0.0) for hs in heads]
    o_intra = [_bdot(att[h], v[:, hs]) for h, hs in enumerate(heads)]
    d_st = [[_dot_tn(v[rs, hs], k_st[rs, hs]) for rs in chunks] for hs in heads]
    outs = []
    for h, hs in enumerate(heads):
        st = st_s[h]
        st_in = [None] * n_chunks
        for c in order:
            st_in[c] = st
            st = st * e_tot[c, :, hs] + d_st[h][c]
        st_s[h] = st
        o_inter = [_dot_nt(q_in[rs, hs], st_in[c]) for c, rs in enumerate(chunks)]
        outs.append(o_intra[h] + jnp.concatenate(o_inter, axis=0))

    if post is None:
        for h in range(N_HEADS):
            o_ref[0, :, h * HEAD_DIM:(h + 1) * HEAD_DIM] = outs[h]
        return
    of_ref, og_ref, gb_ref, hgn_ref, wpb_ref = post
    parts = []
    for h in range(N_HEADS):
        hs = slice(h * HEAD_DIM, (h + 1) * HEAD_DIM)
        o = of_ref[0, :, hs] + outs[h]
        parts.append(o * lax.rsqrt(jnp.mean(o * o, axis=-1, keepdims=True) + EPS) * hgn_ref[:, hs])
    yb = jnp.concatenate(parts, axis=1)
    o_ref[0] = _sigmoid(gb_ref[0]) * _bdot(yb * _silu(og_ref[0]), wpb_ref[...])


def _hg_call(zh, f_slab, lb, s0, bsz, reverse, post=None):
    t, width = zh.shape[1], zh.shape[2]
    seq = t // bsz
    assert seq // GRID_W == CHUNK
    n_chunks = 8
    rows = n_chunks * CHUNK
    n_grp = seq // rows
    z4 = zh.reshape(zh.shape[0], bsz, seq, width)

    def grp(g):
        return (n_grp - 1 - g) if reverse else g

    nb = 2 if bsz % 2 == 0 else 1

    def slab(j):
        return pl.BlockSpec((1, nb, rows, width), lambda b, g: (j, b, grp(g), 0))

    tok = pl.BlockSpec((nb, rows, width), lambda b, g: (b, grp(g), 0))
    row = pl.BlockSpec((1, width), lambda b, g: (0, 0))
    in_specs = [slab(0), slab(f_slab), slab(3), row,
                pl.BlockSpec((nb, N_HEADS, HEAD_DIM, HEAD_DIM), lambda b, g: (b, 0, 0, 0))]
    args = [z4, z4, z4, lb.reshape(1, width), s0]
    if post is not None:
        o_other, hg_norm_g, w_proj_b = post
        in_specs += [tok, slab(4), slab(5), row, pl.BlockSpec((width, width), lambda b, g: (0, 0))]
        args += [o_other, z4, z4, hg_norm_g.reshape(1, width), w_proj_b.astype(BF16)]

    def kern(q_ref, f_ref, v_ref, lb_ref, s0_ref, *rest):
        *extra, o_ref, st_s = rest

        @pl.when(pl.program_id(1) == 0)
        def _():
            st_s[...] = s0_ref[...]

        for bi in range(nb):
            one = pl.ds(bi, 1)
            post_refs = None
            if extra:
                of_ref, og_ref, gb_ref, hgn_ref, wpb_ref = extra
                post_refs = (of_ref.at[one], og_ref.at[0, one], gb_ref.at[0, one], hgn_ref, wpb_ref)
            _hg_kernel(q_ref.at[0, one], f_ref.at[0, one], v_ref.at[0, one], lb_ref, o_ref.at[one],
                       st_s.at[bi], reverse=reverse, n_chunks=n_chunks, post=post_refs)

    return pl.pallas_call(
        kern,
        grid=(bsz // nb, n_grp),
        in_specs=in_specs,
        out_specs=tok,
        out_shape=jax.ShapeDtypeStruct((bsz, seq, width), F32),
        scratch_shapes=[pltpu.VMEM((nb, N_HEADS, HEAD_DIM, HEAD_DIM), F32)],
        compiler_params=_params("arbitrary", "arbitrary"),
        name="hg_bwd" if reverse else "hg_fwd",
    )(*args)


def _merge_kernel(x_ref, ya_ref, ay_ref, ga_ref, mb_ref,
                  gate_ref, shift_ref, scale_ref, n2_ref,
                  wpa_ref, wo_ref, wr_ref, br_ref,
                  x1_ref, h2_ref, route_ref, cnt_ref, run_s):
    tm = x_ref.shape[1]

    @pl.when((pl.program_id(0) == 0) & (pl.program_id(1) == 0))
    def _():
        run_s[...] = jnp.zeros_like(run_s)

    pa = _bdot(ya_ref[0] * jax.nn.gelu(ay_ref[0, 0]), wpa_ref[...])
    merged = _sigmoid(ga_ref[0, 0]) * pa + mb_ref[0]
    x1 = x_ref[0] + gate_ref[0] * _bdot(merged, wo_ref[...])
    x1_ref[0] = x1
    h2 = _norm_mod(x1, n2_ref[...], shift_ref[0], scale_ref[0])
    h2_ref[0] = h2

    h_hi, h_lo, _ = _split3(h2)
    w_hi, w_lo, _ = _split3(wr_ref[...])
    dot = functools.partial(jnp.dot, preferred_element_type=F32)
    logit = dot(h_hi, w_hi) + (dot(h_hi, w_lo) + dot(h_lo, w_hi)) + br_ref[...]
    col = lax.broadcasted_iota(jnp.int32, (tm, LANES), 1)
    colf = col.astype(F32)
    neg = -jnp.inf

    def first_at(mask):
        return -jnp.max(jnp.where(mask, -colf, -1e9), axis=1, keepdims=True)

    is_g = (col >= N_EXPERTS) & (col < N_EXPERTS + N_GROUPS)
    lgm = jnp.where(is_g, logit, neg)
    mg = jnp.max(lgm, axis=1, keepdims=True)
    pg = 1.0 / jnp.sum(jnp.exp(lgm - mg), axis=1, keepdims=True)
    gi = first_at(is_g & (logit == mg)) - float(N_EXPERTS)
    is_e = (col < N_EXPERTS) & ((col // EXP_PER_GROUP).astype(F32) == gi)
    lem = jnp.where(is_e, logit, neg)
    ee = jnp.exp(lem - jnp.max(lem, axis=1, keepdims=True))
    pe = ee / jnp.sum(ee, axis=1, keepdims=True)
    p1 = jnp.max(jnp.where(is_e, pe, -1.0), axis=1, keepdims=True)
    i1 = first_at(is_e & (pe == p1))
    rest = is_e & (colf != i1)
    p2 = jnp.max(jnp.where(rest, pe, -1.0), axis=1, keepdims=True)
    i2 = first_at(rest & (pe == p2))
    den = p1 + p2

    r_i = lax.broadcasted_iota(jnp.int32, (tm, tm), 0)
    c_i = lax.broadcasted_iota(jnp.int32, (tm, tm), 1)
    earlier = (c_i < r_i).astype(BF16)
    oh1 = (colf == i1).astype(F32)
    oh2 = (colf == i2).astype(F32)
    run = run_s[...]
    rank1 = jnp.sum(oh1 * (_bdot(earlier, oh1) + run), axis=1, keepdims=True)
    run = run + jnp.sum(oh1, axis=0, keepdims=True)
    rank2 = jnp.sum(oh2 * (_bdot(earlier, oh2) + run), axis=1, keepdims=True)
    run = run + jnp.sum(oh2, axis=0, keepdims=True)
    run_s[...] = run
    cnt_ref[...] = run
    vals = (i1, i2, pg * p1 / den, pg * p2 / den, rank1, rank2)
    route_ref[0] = sum(jnp.where(col == k, v, 0.0) for k, v in enumerate(vals))


def _merge_call(x, ya, z, mb, gate, shift, scale, norm2_g, w_proj_a, w_out, w_router, b_router):
    bsz, seq, d = x.shape
    tm = min(seq, 512)
    z4 = z.reshape(z.shape[0], bsz, seq, d)
    tok = pl.BlockSpec((1, tm, d), lambda b, i: (b, i, 0))

    def slab(j):
        return pl.BlockSpec((1, 1, tm, d), lambda b, i: (j, b, i, 0))

    vec = pl.BlockSpec((1, 1, d), lambda b, i: (b, 0, 0))
    row = pl.BlockSpec((1, d), lambda b, i: (0, 0))
    wsp = pl.BlockSpec((d, d), lambda b, i: (0, 0))
    return pl.pallas_call(
        _merge_kernel,
        grid=(bsz, seq // tm),
        in_specs=[tok, tok, slab(1), slab(2), tok,
                  vec, vec, vec, row, wsp, wsp,
                  pl.BlockSpec((d, LANES), lambda b, i: (0, 0)),
                  pl.BlockSpec((1, LANES), lambda b, i: (0, 0))],
        out_specs=[tok, tok, pl.BlockSpec((1, tm, LANES), lambda b, i: (b, i, 0)),
                   pl.BlockSpec((1, LANES), lambda b, i: (0, 0))],
        out_shape=[jax.ShapeDtypeStruct((bsz, seq, d), F32),
                   jax.ShapeDtypeStruct((bsz, seq, d), F32),
                   jax.ShapeDtypeStruct((bsz, seq, LANES), F32),
                   jax.ShapeDtypeStruct((1, LANES), F32)],
        scratch_shapes=[pltpu.VMEM((1, LANES), F32)],
        compiler_params=_params("arbitrary", "arbitrary"),
        name="merge",
    )(x, ya, z4, z4, mb, gate, shift, scale, norm2_g.reshape(1, d),
      w_proj_a.astype(BF16), w_out.astype(BF16), w_router, b_router)


MOE_TILE = 256


def _row_copy(src_ref, src_row, dst_ref, dst_row, sem):
    return pltpu.make_async_copy(src_ref.at[pl.ds(src_row, 1), :], dst_ref.at[pl.ds(dst_row, 1), :], sem)


ROW_DMA_UNROLL = 8


def _start_and_wait_rows(copies, n_rows):
    def start(r, carry):
        for k, cp in enumerate(copies(r)):
            cp.start(priority=k)
        return carry

    def wait(r, carry):
        for cp in copies(r):
            cp.wait()
        return carry

    lax.fori_loop(0, n_rows, start, 0, unroll=ROW_DMA_UNROLL)
    lax.fori_loop(0, n_rows, wait, 0, unroll=ROW_DMA_UNROLL)


def _dispatch_kernel(zrow_ref, zon_ref, pos_ref, h_ref, xs_ref, zero_s, sem, zsem):
    tm = h_ref.shape[0]

    @pl.when(pl.program_id(0) == 0)
    def _():
        zero_s[...] = jnp.zeros_like(zero_s)

        def fill(j):
            row0 = pl.multiple_of(zrow_ref[j], MOE_TILE)
            return pltpu.make_async_copy(zero_s, xs_ref.at[pl.ds(row0, MOE_TILE), :], zsem)

        for j in range(zrow_ref.shape[0]):
            @pl.when(zon_ref[j] == 1)
            def _(j=j):
                fill(j).start()

        for j in range(zrow_ref.shape[0]):
            @pl.when(zon_ref[j] == 1)
            def _(j=j):
                fill(j).wait()

    def copies(r):
        return [_row_copy(h_ref, r, xs_ref, pos_ref[0, 0, k * tm + r], sem) for k in range(2)]

    _start_and_wait_rows(copies, tm)


def _dispatch_call(pos, zero_rows, zero_on, h2, n_slots):
    t, d = h2.shape
    tm = pos.shape[2] // 2
    return pl.pallas_call(
        _dispatch_kernel,
        grid_spec=pltpu.PrefetchScalarGridSpec(
            num_scalar_prefetch=2,
            grid=(t // tm,),
            in_specs=[pl.BlockSpec((1, 1, 2 * tm), lambda i, zr, zo: (i, 0, 0), memory_space=pltpu.SMEM),
                      pl.BlockSpec((tm, d), lambda i, zr, zo: (i, 0))],
            out_specs=pl.BlockSpec(memory_space=pltpu.HBM),
            scratch_shapes=[pltpu.VMEM((MOE_TILE, d), F32), pltpu.SemaphoreType.DMA(()),
                            pltpu.SemaphoreType.DMA(())]),
        out_shape=jax.ShapeDtypeStruct((n_slots, d), F32),
        compiler_params=_params("arbitrary"),
        name="dispatch",
    )(zero_rows, zero_on, pos, h2)


def _experts_kernel(te_ref, nu_ref, xs_ref, w1_ref, w3_ref, w2_ref, ys_ref, w1_s, w3_s, w2_s):
    i = pl.program_id(0)

    @pl.when(i < nu_ref[0])
    def _():
        @pl.when((i == 0) | (te_ref[i] != te_ref[jnp.maximum(i - 1, 0)]))
        def _():
            w1_s[...] = w1_ref[0].astype(BF16)
            w3_s[...] = w3_ref[0].astype(BF16)
            w2_s[...] = w2_ref[0].astype(BF16)

        xb = xs_ref[...].astype(BF16)
        up = _silu(_bdot(xb, w1_s[...])) * _bdot(xb, w3_s[...])
        ys_ref[...] = _bdot(up, w2_s[...])

    @pl.when(i >= nu_ref[0])
    def _():
        ys_ref[...] = jnp.zeros_like(ys_ref)


def _experts_call(tile_expert, n_used, xs, w1, w3, w2):
    n_slots, d = xs.shape
    _, _, d_e = w1.shape
    n_tiles = n_slots // MOE_TILE
    tok = pl.BlockSpec((MOE_TILE, d), lambda i, te, nu: (jnp.minimum(i, nu[0] - 1), 0))
    return pl.pallas_call(
        _experts_kernel,
        grid_spec=pltpu.PrefetchScalarGridSpec(
            num_scalar_prefetch=2,
            grid=(n_tiles,),
            in_specs=[tok,
                      pl.BlockSpec((1, d, d_e), lambda i, te, nu: (te[i], 0, 0)),
                      pl.BlockSpec((1, d, d_e), lambda i, te, nu: (te[i], 0, 0)),
                      pl.BlockSpec((1, d_e, d), lambda i, te, nu: (te[i], 0, 0))],
            out_specs=pl.BlockSpec((MOE_TILE, d), lambda i, te, nu: (i, 0)),
            scratch_shapes=[pltpu.VMEM((d, d_e), BF16), pltpu.VMEM((d, d_e), BF16), pltpu.VMEM((d_e, d), BF16)]),
        out_shape=jax.ShapeDtypeStruct((n_slots, d), F32),
        compiler_params=_params("arbitrary"),
        name="experts",
    )(tile_expert, n_used, xs, w1, w3, w2)


def _combine_kernel(pos_ref, route_ref, x1_ref, gate_ref, fg_ref, ys_ref, o_ref, y1_s, y2_s, sem):
    tm = x1_ref.shape[0]

    def copies(r):
        return [_row_copy(ys_ref, pos_ref[0, 0, r], y1_s, r, sem),
                _row_copy(ys_ref, pos_ref[0, 0, tm + r], y2_s, r, sem)]

    _start_and_wait_rows(copies, tm)
    route = route_ref[...]
    y = route[:, 2:3] * y1_s[...] + route[:, 3:4] * y2_s[...]
    x2 = x1_ref[...] + gate_ref[0] * y
    o_ref[...] = x2 * lax.rsqrt(jnp.mean(x2 * x2, axis=-1, keepdims=True) + EPS) * fg_ref[...]


def _combine_call(pos, route, x1, gate, final_g, ys, seq):
    t, d = x1.shape
    tm = pos.shape[2] // 2
    per_b = seq // tm
    tok = pl.BlockSpec((tm, d), lambda i: (i, 0))
    return pl.pallas_call(
        _combine_kernel,
        grid=(t // tm,),
        in_specs=[pl.BlockSpec((1, 1, 2 * tm), lambda i: (i, 0, 0), memory_space=pltpu.SMEM),
                  pl.BlockSpec((tm, LANES), lambda i: (i, 0)), tok,
                  pl.BlockSpec((1, 1, d), lambda i: (i // per_b, 0, 0)),
                  pl.BlockSpec((1, d), lambda i: (0, 0)),
                  pl.BlockSpec(memory_space=pltpu.HBM)],
        out_specs=tok,
        out_shape=jax.ShapeDtypeStruct((t, d), F32),
        scratch_shapes=[pltpu.VMEM((tm, d), F32), pltpu.VMEM((tm, d), F32), pltpu.SemaphoreType.DMA(())],
        compiler_params=_params("arbitrary"),
        name="combine",
    )(pos, route, x1, gate, final_g.reshape(1, d), ys)


def _moe_plan(route, cnt, tm):
    t = route.shape[0]
    counts = cnt[0, :N_EXPERTS].astype(jnp.int32)
    n_tiles_e = (counts + MOE_TILE - 1) // MOE_TILE
    tile_end = jnp.cumsum(n_tiles_e)
    offsets = (tile_end - n_tiles_e) * MOE_TILE
    n_used = tile_end[-1]
    experts = jnp.arange(N_EXPERTS, dtype=jnp.int32)

    def slot(eid, rank):
        base = jnp.sum(jnp.where(eid.astype(jnp.int32)[:, None] == experts[None, :], offsets[None, :], 0), axis=1)
        return base + rank.astype(jnp.int32)

    pos1 = slot(route[:, 0], route[:, 4]).reshape(t // tm, tm)
    pos2 = slot(route[:, 1], route[:, 5]).reshape(t // tm, tm)
    pos = jnp.concatenate([pos1, pos2], axis=1)[:, None, :]
    n_tiles = (2 * t) // MOE_TILE + N_EXPERTS
    tiles = jnp.arange(n_tiles, dtype=jnp.int32)
    tile_expert = jnp.sum(jnp.minimum(tiles, n_used - 1)[:, None] >= tile_end[None, :], axis=1).astype(jnp.int32)
    has_pad = counts % MOE_TILE != 0
    spare = (2 * t) // MOE_TILE + experts
    zero_rows = jnp.concatenate([jnp.where(has_pad, (tile_end - 1) * MOE_TILE, 0), spare * MOE_TILE])
    zero_on = jnp.concatenate([has_pad, spare >= n_used]).astype(jnp.int32)
    return (pos, tile_expert, n_used.reshape(1).astype(jnp.int32), zero_rows.astype(jnp.int32), zero_on,
            n_tiles * MOE_TILE)


def kernel(x, c, ctx, c_ctx, w_mod, b_mod, norm1_g, w_in, conv_w, conv_b, lru_wa, lru_ba, lru_wx, lru_bx,
           lru_lam, hg_lb, hg_norm_g, w_proj_a, w_proj_b, w_out, norm2_g, w_rg, b_rg, w_re, b_re,
           w1, w3, w2, final_g):
    bsz, seq, d = x.shape
    seq_c = ctx.shape[1]
    l = 0

    pad = (-(bsz + 1)) % SUBLANES
    cc = jnp.concatenate([c, c_ctx[None, :], jnp.zeros((pad, d), F32)], axis=0)
    mod = _mod_call(cc, w_mod[l], b_mod[l])
    mx = [mod[:bsz, None, k * d:(k + 1) * d] for k in range(6)]
    mc = [jnp.broadcast_to(mod[bsz:bsz + 1, None, k * d:(k + 1) * d], (bsz, 1, d)) for k in range(2)]
    lb = jax.nn.softmax(hg_lb.astype(F32), axis=0)[l]

    hx = _norm_call(x, norm1_g[l], mx[0], mx[1])
    hc = _norm_call(ctx, norm1_g[l], mc[0], mc[1]).reshape(bsz * seq_c, d)
    rows = seq // GRID_W
    hx_cm = hx.reshape(bsz, rows, GRID_W, d).swapaxes(1, 2).reshape(bsz * seq, d)
    hx = hx.reshape(bsz * seq, d)
    z = _proj_call(hx, w_in[l], lambda j: jnp.where(j < 2, j, 7), 3, "proj_r")
    zh = _proj_call(hx_cm, w_in[l], lambda j: jnp.where(j < 5, j + 2, 8), 6, "proj_h")
    zc = _proj_call(hc, w_in[l], lambda j: jnp.where(j > 0, j + 2, 0), 4, "proj_c")

    ya = _lru_call(z, zc, conv_w[l], conv_b[l], lru_wa[l], lru_wx[l], lru_ba[l], lru_bx[l], lru_lam[l], bsz)
    s_f, s_b = _hgc_call(zc, lb, bsz)
    o_f = _hg_call(zh, 1, lb, s_f, bsz, reverse=False)
    mb_cm = _hg_call(zh, 2, lb, s_b, bsz, reverse=True, post=(o_f, hg_norm_g[l], w_proj_b[l]))
    mb = mb_cm.reshape(bsz, GRID_W, rows, d).swapaxes(1, 2).reshape(bsz, seq, d)

    w_router = jnp.zeros((d, LANES), F32).at[:, :N_EXPERTS].set(w_re[l])
    w_router = w_router.at[:, N_EXPERTS:N_EXPERTS + N_GROUPS].set(w_rg[l])
    b_router = jnp.zeros((1, LANES), F32).at[0, :N_EXPERTS].set(b_re[l])
    b_router = b_router.at[0, N_EXPERTS:N_EXPERTS + N_GROUPS].set(b_rg[l])
    x1, h2, route, cnt = _merge_call(x, ya, z, mb, mx[2], mx[3], mx[4], norm2_g[l],
                                     w_proj_a[l], w_out[l], w_router, b_router)

    route = route.reshape(bsz * seq, LANES)
    pos, tile_expert, n_used, zero_rows, zero_on, n_slots = _moe_plan(route, cnt, min(seq, 512))
    xs = _dispatch_call(pos, zero_rows, zero_on, h2.reshape(bsz * seq, d), n_slots)
    ys = _experts_call(tile_expert, n_used, xs, w1[l], w3[l], w2[l])
    out = _combine_call(pos, route, x1.reshape(bsz * seq, d), mx[5], final_g, ys, seq)
    return out.reshape(bsz, seq, d)
```

```python
import functools

import jax
import jax.numpy as jnp
from jax import lax
from jax.experimental import pallas as pl
from jax.experimental.pallas import tpu as pltpu

F32 = jnp.float32
BF16 = jnp.bfloat16
HIGHEST = lax.Precision.HIGHEST

EPS = 1e-6
GRID_W = 64
CHUNK = 32
LRU_C = 8.0
CONV_W = 4
N_HEADS = 8
HEAD_DIM = 128
N_GROUPS = 4
EXP_PER_GROUP = 8
N_EXPERTS = N_GROUPS * EXP_PER_GROUP
LANES = 128
SUBLANES = 8
VMEM_LIMIT = 56 * 1024 * 1024


def _params(*sem):
    return pltpu.CompilerParams(dimension_semantics=sem, vmem_limit_bytes=VMEM_LIMIT)


def _sigmoid(x):
    return 0.5 * jnp.tanh(0.5 * x) + 0.5


def _silu(x):
    return x * _sigmoid(x)


def _bdot(a, b):
    return jnp.dot(a.astype(BF16), b.astype(BF16), preferred_element_type=F32)


def _mod_kernel(c_ref, w_ref, b_ref, o_ref):
    o_ref[...] = jnp.dot(_silu(c_ref[...]), w_ref[...], precision=HIGHEST,
                         preferred_element_type=F32) + b_ref[...]


def _mod_call(cc, w_mod, b_mod):
    rows, d = cc.shape
    n = w_mod.shape[1]
    tn = 1024
    return pl.pallas_call(
        _mod_kernel,
        grid=(n // tn,),
        in_specs=[pl.BlockSpec((rows, d), lambda j: (0, 0)),
                  pl.BlockSpec((d, tn), lambda j: (0, j)),
                  pl.BlockSpec((1, tn), lambda j: (0, j))],
        out_specs=pl.BlockSpec((rows, tn), lambda j: (0, j)),
        out_shape=jax.ShapeDtypeStruct((rows, n), F32),
        compiler_params=_params("arbitrary"),
        name="mod",
    )(cc, w_mod, b_mod.reshape(1, n))


def _norm_mod(x, g, shift, scale):
    y = x * lax.rsqrt(jnp.mean(x * x, axis=-1, keepdims=True) + EPS) * g
    return y * (1.0 + scale) + shift


def _norm_kernel(x_ref, g_ref, shift_ref, scale_ref, o_ref):
    o_ref[0] = _norm_mod(x_ref[0], g_ref[...], shift_ref[0], scale_ref[0]).astype(o_ref.dtype)


def _norm_call(x, g, shift, scale):
    b, l, d = x.shape
    tl = min(l, 512)
    vec = pl.BlockSpec((1, 1, d), lambda i, j: (i, 0, 0))
    return pl.pallas_call(
        _norm_kernel,
        grid=(b, l // tl),
        in_specs=[pl.BlockSpec((1, tl, d), lambda i, j: (i, j, 0)),
                  pl.BlockSpec((1, d), lambda i, j: (0, 0)), vec, vec],
        out_specs=pl.BlockSpec((1, tl, d), lambda i, j: (i, j, 0)),
        out_shape=jax.ShapeDtypeStruct((b, l, d), BF16),
        compiler_params=_params("arbitrary", "arbitrary"),
        name="norm",
    )(x, g.reshape(1, d), shift, scale)


def _proj_kernel(a_ref, w_ref, o_ref):
    o_ref[0] = jnp.dot(a_ref[...], w_ref[...].astype(BF16), preferred_element_type=F32)


def _proj_call(a, w, col_of, n_out, name):
    t, k = a.shape
    tn = 1024
    tm = min(t, 2048)
    return pl.pallas_call(
        _proj_kernel,
        grid=(t // tm, n_out),
        in_specs=[pl.BlockSpec((tm, k), lambda i, j: (i, 0)),
                  pl.BlockSpec((k, tn), lambda i, j: (0, col_of(j)))],
        out_specs=pl.BlockSpec((1, tm, tn), lambda i, j: (j, i, 0)),
        out_shape=jax.ShapeDtypeStruct((n_out, t, tn), F32),
        compiler_params=_params("arbitrary", "arbitrary"),
        name=name,
    )(a, w)


def _shift_rows(x, k):
    n = x.shape[0]
    row = lax.broadcasted_iota(jnp.int32, x.shape, 0)
    rolled = pltpu.roll(x, k % n, 0)
    keep = (row >= k) if k > 0 else (row < n + k)
    return jnp.where(keep, rolled, 0.0)


def _lru_kernel(ax_ref, axc_ref, cw_ref, cb_ref, wa_ref, wx_ref, ba_ref, bx_ref, lam_ref,
                o_ref, a_s, b_s):
    seq, width = ax_ref.shape[1], ax_ref.shape[2]
    seq_c = axc_ref.shape[1]
    heads = width // HEAD_DIM

    def conv(x):
        acc = _shift_rows(x, 2) * cw_ref[0:1, :]
        acc = acc + _shift_rows(x, 1) * cw_ref[1:2, :]
        acc = acc + x * cw_ref[2:3, :]
        acc = acc + _shift_rows(x, -1) * cw_ref[3:4, :]
        return acc + cb_ref[...]

    def coeffs(u, d, h):
        hs = slice(h * HEAD_DIM, (h + 1) * HEAD_DIM)
        ub = u.astype(BF16)
        r = _sigmoid(_bdot(ub, wa_ref[d, h]) + ba_ref[d:d + 1, hs])
        i = _sigmoid(_bdot(ub, wx_ref[d, h]) + bx_ref[d:d + 1, hs])
        log_a = r * (-LRU_C * jax.nn.softplus(-lam_ref[d:d + 1, hs]))
        a = jnp.exp(log_a)
        x = 1.0 - a * a
        return a, jnp.where(x > 0.0, x * lax.rsqrt(x), 0.0) * (i * u)

    u_c = conv(axc_ref[0])
    u_x = conv(ax_ref[0])
    n_blk = seq // SUBLANES
    row8 = lax.broadcasted_iota(jnp.int32, (SUBLANES, width), 0)

    for d, reverse in enumerate((False, True)):
        edge = u_c[seq_c - SUBLANES:] if reverse else u_c[:SUBLANES]
        pick = SUBLANES - 1 if reverse else 0
        h0 = jnp.concatenate([coeffs(edge[:, h * HEAD_DIM:(h + 1) * HEAD_DIM], d, h)[1][pick:pick + 1]
                              for h in range(heads)], axis=1)
        for h in range(heads):
            hs = slice(h * HEAD_DIM, (h + 1) * HEAD_DIM)
            a, b = coeffs(u_x[:, hs], d, h)
            a_s[:, hs] = a
            b_s[:, hs] = b

        def body(i, h_prev, reverse=reverse):
            blk = (n_blk - 1 - i) if reverse else i
            rows = pl.ds(pl.multiple_of(blk * SUBLANES, SUBLANES), SUBLANES)
            a = a_s[rows, :]
            b = b_s[rows, :]
            for s in (1, 2, 4):
                if reverse:
                    keep = row8 < SUBLANES - s
                    sh = SUBLANES - s
                else:
                    keep = row8 >= s
                    sh = s
                a_sh = jnp.where(keep, pltpu.roll(a, sh, 0), 1.0)
                b_sh = jnp.where(keep, pltpu.roll(b, sh, 0), 0.0)
                b = a * b_sh + b
                a = a * a_sh
            hh = b + a * h_prev
            if reverse:
                o_ref[0, rows, :] = o_ref[0, rows, :] + hh
                return hh[0:1, :]
            o_ref[0, rows, :] = hh
            return hh[SUBLANES - 1:SUBLANES, :]

        lax.fori_loop(0, n_blk, body, h0, unroll=8)


def _lru_call(z, zc, conv_w, conv_b, wa, wx, ba, bx, lam, bsz):
    t = z.shape[1]
    seq = t // bsz
    seq_c = zc.shape[1] // bsz
    width = 2 * HEAD_DIM
    n_w = (N_HEADS * HEAD_DIM) // width
    z4 = z.reshape(z.shape[0], bsz, seq, z.shape[2])
    zc4 = zc.reshape(zc.shape[0], bsz, seq_c, zc.shape[2])
    vec2 = pl.BlockSpec((2, width), lambda b, w: (0, w))
    wspec = pl.BlockSpec((2, width // HEAD_DIM, HEAD_DIM, HEAD_DIM), lambda b, w: (0, w, 0, 0))

    def kern(ax_ref, axc_ref, *rest):
        _lru_kernel(ax_ref.at[0], axc_ref.at[0], *rest)

    return pl.pallas_call(
        kern,
        grid=(bsz, n_w),
        in_specs=[pl.BlockSpec((1, 1, seq, width), lambda b, w: (0, b, 0, w)),
                  pl.BlockSpec((1, 1, seq_c, width), lambda b, w: (0, b, 0, w)),
                  pl.BlockSpec((CONV_W, width), lambda b, w: (0, w)),
                  pl.BlockSpec((1, width), lambda b, w: (0, w)),
                  wspec, wspec, vec2, vec2, vec2],
        out_specs=pl.BlockSpec((1, seq, width), lambda b, w: (b, 0, w)),
        out_shape=jax.ShapeDtypeStruct((bsz, seq, N_HEADS * HEAD_DIM), F32),
        scratch_shapes=[pltpu.VMEM((seq, width), F32), pltpu.VMEM((seq, width), F32)],
        compiler_params=_params("arbitrary", "arbitrary"),
        name="lru",
    )(z4, zc4, conv_w, conv_b.reshape(1, -1), wa, wx, ba, bx, lam)


def _split3(x):
    x1 = x.astype(BF16)
    r1 = x - x1.astype(F32)
    x2 = r1.astype(BF16)
    x3 = (r1 - x2.astype(F32)).astype(BF16)
    return x1, x2, x3


def _tri_cumsum(tri, x):
    x1, x2, x3 = _split3(x)
    dot = functools.partial(jnp.dot, preferred_element_type=F32)
    return dot(tri, x1) + dot(tri, x2) + dot(tri, x3)


def _hg_gates(f, lb):
    g = lb + (1.0 - lb) * _sigmoid(f)
    return jnp.log(g), 1.0 - g


def _dot_tn(a, b):
    return lax.dot_general(a.astype(BF16), b.astype(BF16), (((0,), (0,)), ((), ())),
                           preferred_element_type=F32)


def _dot_nt(a, b):
    return lax.dot_general(a.astype(BF16), b.astype(BF16), (((1,), (1,)), ((), ())),
                           preferred_element_type=F32)


def _hgc_kernel(ff_ref, fb_ref, v_ref, lb_ref, sf_ref, sb_ref):
    n = ff_ref.shape[1]
    row = lax.broadcasted_iota(jnp.int32, (n, n), 0)
    col = lax.broadcasted_iota(jnp.int32, (n, n), 1)
    tri = (col <= row).astype(BF16)
    lb = lb_ref[...]
    v = v_ref[0]
    lg, k = _hg_gates(ff_ref[0], lb)
    lc = _tri_cumsum(tri, lg)
    kf = k * jnp.exp(lc[n - 1:n, :] - lc)
    lg, k = _hg_gates(fb_ref[0], lb)
    lc = _tri_cumsum(tri, lg)
    kb = k * jnp.exp(lc - lg)
    for h in range(N_HEADS):
        hs = slice(h * HEAD_DIM, (h + 1) * HEAD_DIM)
        sf_ref[0, h] = _dot_tn(v[:, hs], kf[:, hs])
        sb_ref[0, h] = _dot_tn(v[:, hs], kb[:, hs])


def _hgc_call(zc, lb, bsz):
    seq_c = zc.shape[1] // bsz
    width = zc.shape[2]
    zc4 = zc.reshape(zc.shape[0], bsz, seq_c, width)

    def slab(j):
        return pl.BlockSpec((1, 1, seq_c, width), lambda b: (j, b, 0, 0))

    def kern(ff_ref, fb_ref, v_ref, lb_ref, sf_ref, sb_ref):
        _hgc_kernel(ff_ref.at[0], fb_ref.at[0], v_ref.at[0], lb_ref, sf_ref, sb_ref)

    sspec = pl.BlockSpec((1, N_HEADS, HEAD_DIM, HEAD_DIM), lambda b: (b, 0, 0, 0))
    sshape = jax.ShapeDtypeStruct((bsz, N_HEADS, HEAD_DIM, HEAD_DIM), F32)
    return pl.pallas_call(
        kern,
        grid=(bsz,),
        in_specs=[slab(1), slab(2), slab(3), pl.BlockSpec((1, width), lambda b: (0, 0))],
        out_specs=[sspec, sspec],
        out_shape=[sshape, sshape],
        compiler_params=_params("arbitrary"),
        name="hgc",
    )(zc4, zc4, zc4, lb.reshape(1, width))


def _hg_kernel(q_ref, f_ref, v_ref, lb_ref, o_ref, st_s, *, reverse, n_chunks, post):
    width = N_HEADS * HEAD_DIM
    rows = n_chunks * CHUNK

    row = lax.broadcasted_iota(jnp.int32, (rows, rows), 0)
    col = lax.broadcasted_iota(jnp.int32, (rows, rows), 1)
    same = (row // CHUNK) == (col // CHUNK)
    causal = same & ((col >= row) if reverse else (col <= row))
    last, mid = (0, CHUNK - 1 - CHUNK // 2) if reverse else (CHUNK - 1, CHUNK // 2)

    def chunked(t):
        return t.reshape(n_chunks, CHUNK, width)

    def flat(t):
        return t.reshape(rows, width)

    def chunk_cumsum(x):
        pos = lax.broadcasted_iota(jnp.int32, x.shape, 0) % CHUNK
        s = 1
        while s < CHUNK:
            if reverse:
                x = x + jnp.where(pos < CHUNK - s, pltpu.roll(x, rows - s, 0), 0.0)
            else:
                x = x + jnp.where(pos >= s, pltpu.roll(x, s, 0), 0.0)
            s *= 2
        return x

    q = chunked(_silu(q_ref[0]))
    v = v_ref[0]
    lg, k = _hg_gates(f_ref[0], lb_ref[...])
    k = chunked(k)
    lc = chunked(chunk_cumsum(lg))
    ltot = lc[:, last:last + 1, :]
    lmid = lc[:, mid:mid + 1, :]
    e_tot = jnp.exp(ltot)
    q_at = q * jnp.exp(lc - lmid)
    k_at = k * jnp.exp(lmid - lc)
    q_in = flat(q_at * jnp.exp(lmid))
    k_st = flat(k_at * jnp.exp(ltot - lmid))
    q_at = flat(q_at)
    k_at = flat(k_at)

    heads = [slice(h * HEAD_DIM, (h + 1) * HEAD_DIM) for h in range(N_HEADS)]
    chunks = [slice(c * CHUNK, (c + 1) * CHUNK) for c in range(n_chunks)]
    order = list(reversed(range(n_chunks))) if reverse else list(range(n_chunks))
    att = [jnp.where(causal, _dot_nt(q_at[:, hs], k_at[:, hs]), 0.0) for hs in heads]
    o_intra = [_bdot(att[h], v[:, hs]) for h, hs in enumerate(heads)]
    d_st = [[_dot_tn(v[rs, hs], k_st[rs, hs]) for rs in chunks] for hs in heads]
    outs = []
    for h, hs in enumerate(heads):
        st = st_s[h]
        st_in = [None] * n_chunks
        for c in order:
            st_in[c] = st
            st = st * e_tot[c, :, hs] + d_st[h][c]
        st_s[h] = st
        o_inter = [_dot_nt(q_in[rs, hs], st_in[c]) for c, rs in enumerate(chunks)]
        outs.append(o_intra[h] + jnp.concatenate(o_inter, axis=0))

    if post is None:
        for h in range(N_HEADS):
            o_ref[0, :, h * HEAD_DIM:(h + 1) * HEAD_DIM] = outs[h]
        return
    of_ref, og_ref, gb_ref, hgn_ref, wpb_ref = post
    parts = []
    for h in range(N_HEADS):
        hs = slice(h * HEAD_DIM, (h + 1) * HEAD_DIM)
        o = of_ref[0, :, hs] + outs[h]
        parts.append(o * lax.rsqrt(jnp.mean(o * o, axis=-1, keepdims=True) + EPS) * hgn_ref[:, hs])
    yb = jnp.concatenate(parts, axis=1)
    o_ref[0] = _sigmoid(gb_ref[0]) * _bdot(yb * _silu(og_ref[0]), wpb_ref[...])


def _hg_call(zh, f_slab, lb, s0, bsz, reverse, post=None):
    t, width = zh.shape[1], zh.shape[2]
    seq = t // bsz
    assert seq // GRID_W == CHUNK
    n_chunks = 8
    rows = n_chunks * CHUNK
    n_grp = seq // rows
    z4 = zh.reshape(zh.shape[0], bsz, seq, width)

    def grp(g):
        return (n_grp - 1 - g) if reverse else g

    nb = 2 if bsz % 2 == 0 else 1

    def slab(j):
        return pl.BlockSpec((1, nb, rows, width), lambda b, g: (j, b, grp(g), 0))

    tok = pl.BlockSpec((nb, rows, width), lambda b, g: (b, grp(g), 0))
    row = pl.BlockSpec((1, width), lambda b, g: (0, 0))
    in_specs = [slab(0), slab(f_slab), slab(3), row,
                pl.BlockSpec((nb, N_HEADS, HEAD_DIM, HEAD_DIM), lambda b, g: (b, 0, 0, 0))]
    args = [z4, z4, z4, lb.reshape(1, width), s0]
    if post is not None:
        o_other, hg_norm_g, w_proj_b = post
        in_specs += [tok, slab(4), slab(5), row, pl.BlockSpec((width, width), lambda b, g: (0, 0))]
        args += [o_other, z4, z4, hg_norm_g.reshape(1, width), w_proj_b.astype(BF16)]

    def kern(q_ref, f_ref, v_ref, lb_ref, s0_ref, *rest):
        *extra, o_ref, st_s = rest

        @pl.when(pl.program_id(1) == 0)
        def _():
            st_s[...] = s0_ref[...]

        for bi in range(nb):
            one = pl.ds(bi, 1)
            post_refs = None
            if extra:
                of_ref, og_ref, gb_ref, hgn_ref, wpb_ref = extra
                post_refs = (of_ref.at[one], og_ref.at[0, one], gb_ref.at[0, one], hgn_ref, wpb_ref)
            _hg_kernel(q_ref.at[0, one], f_ref.at[0, one], v_ref.at[0, one], lb_ref, o_ref.at[one],
                       st_s.at[bi], reverse=reverse, n_chunks=n_chunks, post=post_refs)

    return pl.pallas_call(
        kern,
        grid=(bsz // nb, n_grp),
        in_specs=in_specs,
        out_specs=tok,
        out_shape=jax.ShapeDtypeStruct((bsz, seq, width), F32),
        scratch_shapes=[pltpu.VMEM((nb, N_HEADS, HEAD_DIM, HEAD_DIM), F32)],
        compiler_params=_params("arbitrary", "arbitrary"),
        name="hg_bwd" if reverse else "hg_fwd",
    )(*args)


def _merge_kernel(x_ref, ya_ref, ay_ref, ga_ref, mb_ref,
                  gate_ref, shift_ref, scale_ref, n2_ref,
                  wpa_ref, wo_ref, wr_ref, br_ref,
                  x1_ref, h2_ref, route_ref, cnt_ref, run_s):
    tm = x_ref.shape[1]

    @pl.when((pl.program_id(0) == 0) & (pl.program_id(1) == 0))
    def _():
        run_s[...] = jnp.zeros_like(run_s)

    pa = _bdot(ya_ref[0] * jax.nn.gelu(ay_ref[0, 0]), wpa_ref[...])
    merged = _sigmoid(ga_ref[0, 0]) * pa + mb_ref[0]
    x1 = x_ref[0] + gate_ref[0] * _bdot(merged, wo_ref[...])
    x1_ref[0] = x1
    h2 = _norm_mod(x1, n2_ref[...], shift_ref[0], scale_ref[0])
    h2_ref[0] = h2

    h_hi, h_lo, _ = _split3(h2)
    w_hi, w_lo, _ = _split3(wr_ref[...])
    dot = functools.partial(jnp.dot, preferred_element_type=F32)
    logit = dot(h_hi, w_hi) + (dot(h_hi, w_lo) + dot(h_lo, w_hi)) + br_ref[...]
    col = lax.broadcasted_iota(jnp.int32, (tm, LANES), 1)
    colf = col.astype(F32)
    neg = -jnp.inf

    def first_at(mask):
        return -jnp.max(jnp.where(mask, -colf, -1e9), axis=1, keepdims=True)

    is_g = (col >= N_EXPERTS) & (col < N_EXPERTS + N_GROUPS)
    lgm = jnp.where(is_g, logit, neg)
    mg = jnp.max(lgm, axis=1, keepdims=True)
    pg = 1.0 / jnp.sum(jnp.exp(lgm - mg), axis=1, keepdims=True)
    gi = first_at(is_g & (logit == mg)) - float(N_EXPERTS)
    is_e = (col < N_EXPERTS) & ((col // EXP_PER_GROUP).astype(F32) == gi)
    lem = jnp.where(is_e, logit, neg)
    ee = jnp.exp(lem - jnp.max(lem, axis=1, keepdims=True))
    pe = ee / jnp.sum(ee, axis=1, keepdims=True)
    p1 = jnp.max(jnp.where(is_e, pe, -1.0), axis=1, keepdims=True)
    i1 = first_at(is_e & (pe == p1))
    rest = is_e & (colf != i1)
    p2 = jnp.max(jnp.where(rest, pe, -1.0), axis=1, keepdims=True)
    i2 = first_at(rest & (pe == p2))
    den = p1 + p2

    r_i = lax.broadcasted_iota(jnp.int32, (tm, tm), 0)
    c_i = lax.broadcasted_iota(jnp.int32, (tm, tm), 1)
    earlier = (c_i < r_i).astype(BF16)
    oh1 = (colf == i1).astype(F32)
    oh2 = (colf == i2).astype(F32)
    run = run_s[...]
    rank1 = jnp.sum(oh1 * (_bdot(earlier, oh1) + run), axis=1, keepdims=True)
    run = run + jnp.sum(oh1, axis=0, keepdims=True)
    rank2 = jnp.sum(oh2 * (_bdot(earlier, oh2) + run), axis=1, keepdims=True)
    run = run + jnp.sum(oh2, axis=0, keepdims=True)
    run_s[...] = run
    cnt_ref[...] = run
    vals = (i1, i2, pg * p1 / den, pg * p2 / den, rank1, rank2)
    route_ref[0] = sum(jnp.where(col == k, v, 0.0) for k, v in enumerate(vals))


def _merge_call(x, ya, z, mb, gate, shift, scale, norm2_g, w_proj_a, w_out, w_router, b_router):
    bsz, seq, d = x.shape
    tm = min(seq, 512)
    z4 = z.reshape(z.shape[0], bsz, seq, d)
    tok = pl.BlockSpec((1, tm, d), lambda b, i: (b, i, 0))

    def slab(j):
        return pl.BlockSpec((1, 1, tm, d), lambda b, i: (j, b, i, 0))

    vec = pl.BlockSpec((1, 1, d), lambda b, i: (b, 0, 0))
    row = pl.BlockSpec((1, d), lambda b, i: (0, 0))
    wsp = pl.BlockSpec((d, d), lambda b, i: (0, 0))
    return pl.pallas_call(
        _merge_kernel,
        grid=(bsz, seq // tm),
        in_specs=[tok, tok, slab(1), slab(2), tok,
                  vec, vec, vec, row, wsp, wsp,
                  pl.BlockSpec((d, LANES), lambda b, i: (0, 0)),
                  pl.BlockSpec((1, LANES), lambda b, i: (0, 0))],
        out_specs=[tok, tok, pl.BlockSpec((1, tm, LANES), lambda b, i: (b, i, 0)),
                   pl.BlockSpec((1, LANES), lambda b, i: (0, 0))],
        out_shape=[jax.ShapeDtypeStruct((bsz, seq, d), F32),
                   jax.ShapeDtypeStruct((bsz, seq, d), F32),
                   jax.ShapeDtypeStruct((bsz, seq, LANES), F32),
                   jax.ShapeDtypeStruct((1, LANES), F32)],
        scratch_shapes=[pltpu.VMEM((1, LANES), F32)],
        compiler_params=_params("arbitrary", "arbitrary"),
        name="merge",
    )(x, ya, z4, z4, mb, gate, shift, scale, norm2_g.reshape(1, d),
      w_proj_a.astype(BF16), w_out.astype(BF16), w_router, b_router)


MOE_TILE = 256


def _row_copy(src_ref, src_row, dst_ref, dst_row, sem):
    return pltpu.make_async_copy(src_ref.at[pl.ds(src_row, 1), :], dst_ref.at[pl.ds(dst_row, 1), :], sem)


ROW_DMA_UNROLL = 8


def _start_and_wait_rows(copies, n_rows):
    def start(r, carry):
        for k, cp in enumerate(copies(r)):
            cp.start(priority=k)
        return carry

    def wait(r, carry):
        for cp in copies(r):
            cp.wait()
        return carry

    lax.fori_loop(0, n_rows, start, 0, unroll=ROW_DMA_UNROLL)
    lax.fori_loop(0, n_rows, wait, 0, unroll=ROW_DMA_UNROLL)


def _dispatch_kernel(zrow_ref, zon_ref, pos_ref, h_ref, xs_ref, zero_s, sem, zsem):
    tm = h_ref.shape[0]

    @pl.when(pl.program_id(0) == 0)
    def _():
        zero_s[...] = jnp.zeros_like(zero_s)

        def fill(j):
            row0 = pl.multiple_of(zrow_ref[j], MOE_TILE)
            return pltpu.make_async_copy(zero_s, xs_ref.at[pl.ds(row0, MOE_TILE), :], zsem)

        for j in range(zrow_ref.shape[0]):
            @pl.when(zon_ref[j] == 1)
            def _(j=j):
                fill(j).start()

        for j in range(zrow_ref.shape[0]):
            @pl.when(zon_ref[j] == 1)
            def _(j=j):
                fill(j).wait()

    def copies(r):
        return [_row_copy(h_ref, r, xs_ref, pos_ref[0, 0, k * tm + r], sem) for k in range(2)]

    _start_and_wait_rows(copies, tm)


def _dispatch_call(pos, zero_rows, zero_on, h2, n_slots):
    t, d = h2.shape
    tm = pos.shape[2] // 2
    return pl.pallas_call(
        _dispatch_kernel,
        grid_spec=pltpu.PrefetchScalarGridSpec(
            num_scalar_prefetch=2,
            grid=(t // tm,),
            in_specs=[pl.BlockSpec((1, 1, 2 * tm), lambda i, zr, zo: (i, 0, 0), memory_space=pltpu.SMEM),
                      pl.BlockSpec((tm, d), lambda i, zr, zo: (i, 0))],
            out_specs=pl.BlockSpec(memory_space=pltpu.HBM),
            scratch_shapes=[pltpu.VMEM((MOE_TILE, d), F32), pltpu.SemaphoreType.DMA(()),
                            pltpu.SemaphoreType.DMA(())]),
        out_shape=jax.ShapeDtypeStruct((n_slots, d), F32),
        compiler_params=_params("arbitrary"),
        name="dispatch",
    )(zero_rows, zero_on, pos, h2)


def _experts_kernel(te_ref, nu_ref, xs_ref, w1_ref, w3_ref, w2_ref, ys_ref, w1_s, w3_s, w2_s):
    i = pl.program_id(0)

    @pl.when(i < nu_ref[0])
    def _():
        @pl.when((i == 0) | (te_ref[i] != te_ref[jnp.maximum(i - 1, 0)]))
        def _():
            w1_s[...] = w1_ref[0].astype(BF16)
            w3_s[...] = w3_ref[0].astype(BF16)
            w2_s[...] = w2_ref[0].astype(BF16)

        xb = xs_ref[...].astype(BF16)
        up = _silu(_bdot(xb, w1_s[...])) * _bdot(xb, w3_s[...])
        ys_ref[...] = _bdot(up, w2_s[...])

    @pl.when(i >= nu_ref[0])
    def _():
        ys_ref[...] = jnp.zeros_like(ys_ref)


def _experts_call(tile_expert, n_used, xs, w1, w3, w2):
    n_slots, d = xs.shape
    _, _, d_e = w1.shape
    n_tiles = n_slots // MOE_TILE
    tok = pl.BlockSpec((MOE_TILE, d), lambda i, te, nu: (jnp.minimum(i, nu[0] - 1), 0))
    return pl.pallas_call(
        _experts_kernel,
        grid_spec=pltpu.PrefetchScalarGridSpec(
            num_scalar_prefetch=2,
            grid=(n_tiles,),
            in_specs=[tok,
                      pl.BlockSpec((1, d, d_e), lambda i, te, nu: (te[i], 0, 0)),
                      pl.BlockSpec((1, d, d_e), lambda i, te, nu: (te[i], 0, 0)),
                      pl.BlockSpec((1, d_e, d), lambda i, te, nu: (te[i], 0, 0))],
            out_specs=pl.BlockSpec((MOE_TILE, d), lambda i, te, nu: (i, 0)),
            scratch_shapes=[pltpu.VMEM((d, d_e), BF16), pltpu.VMEM((d, d_e), BF16), pltpu.VMEM((d_e, d), BF16)]),
        out_shape=jax.ShapeDtypeStruct((n_slots, d), F32),
        compiler_params=_params("arbitrary"),
        name="experts",
    )(tile_expert, n_used, xs, w1, w3, w2)


def _combine_kernel(pos_ref, route_ref, x1_ref, gate_ref, fg_ref, ys_ref, o_ref, y1_s, y2_s, sem):
    tm = x1_ref.shape[0]

    def copies(r):
        return [_row_copy(ys_ref, pos_ref[0, 0, r], y1_s, r, sem),
                _row_copy(ys_ref, pos_ref[0, 0, tm + r], y2_s, r, sem)]

    _start_and_wait_rows(copies, tm)
    route = route_ref[...]
    y = route[:, 2:3] * y1_s[...] + route[:, 3:4] * y2_s[...]
    x2 = x1_ref[...] + gate_ref[0] * y
    o_ref[...] = x2 * lax.rsqrt(jnp.mean(x2 * x2, axis=-1, keepdims=True) + EPS) * fg_ref[...]


def _combine_call(pos, route, x1, gate, final_g, ys, seq):
    t, d = x1.shape
    tm = pos.shape[2] // 2
    per_b = seq // tm
    tok = pl.BlockSpec((tm, d), lambda i: (i, 0))
    return pl.pallas_call(
        _combine_kernel,
        grid=(t // tm,),
        in_specs=[pl.BlockSpec((1, 1, 2 * tm), lambda i: (i, 0, 0), memory_space=pltpu.SMEM),
                  pl.BlockSpec((tm, LANES), lambda i: (i, 0)), tok,
                  pl.BlockSpec((1, 1, d), lambda i: (i // per_b, 0, 0)),
                  pl.BlockSpec((1, d), lambda i: (0, 0)),
                  pl.BlockSpec(memory_space=pltpu.HBM)],
        out_specs=tok,
        out_shape=jax.ShapeDtypeStruct((t, d), F32),
        scratch_shapes=[pltpu.VMEM((tm, d), F32), pltpu.VMEM((tm, d), F32), pltpu.SemaphoreType.DMA(())],
        compiler_params=_params("arbitrary"),
        name="combine",
    )(pos, route, x1, gate, final_g.reshape(1, d), ys)


def _moe_plan(route, cnt, tm):
    t = route.shape[0]
    counts = cnt[0, :N_EXPERTS].astype(jnp.int32)
    n_tiles_e = (counts + MOE_TILE - 1) // MOE_TILE
    tile_end = jnp.cumsum(n_tiles_e)
    offsets = (tile_end - n_tiles_e) * MOE_TILE
    n_used = tile_end[-1]
    experts = jnp.arange(N_EXPERTS, dtype=jnp.int32)

    def slot(eid, rank):
        base = jnp.sum(jnp.where(eid.astype(jnp.int32)[:, None] == experts[None, :], offsets[None, :], 0), axis=1)
        return base + rank.astype(jnp.int32)

    pos1 = slot(route[:, 0], route[:, 4]).reshape(t // tm, tm)
    pos2 = slot(route[:, 1], route[:, 5]).reshape(t // tm, tm)
    pos = jnp.concatenate([pos1, pos2], axis=1)[:, None, :]
    n_tiles = (2 * t) // MOE_TILE + N_EXPERTS
    tiles = jnp.arange(n_tiles, dtype=jnp.int32)
    tile_expert = jnp.sum(jnp.minimum(tiles, n_used - 1)[:, None] >= tile_end[None, :], axis=1).astype(jnp.int32)
    has_pad = counts % MOE_TILE != 0
    spare = (2 * t) // MOE_TILE + experts
    zero_rows = jnp.concatenate([jnp.where(has_pad, (tile_end - 1) * MOE_TILE, 0), spare * MOE_TILE])
    zero_on = jnp.concatenate([has_pad, spare >= n_used]).astype(jnp.int32)
    return (pos, tile_expert, n_used.reshape(1).astype(jnp.int32), zero_rows.astype(jnp.int32), zero_on,
            n_tiles * MOE_TILE)


def kernel(x, c, ctx, c_ctx, w_mod, b_mod, norm1_g, w_in, conv_w, conv_b, lru_wa, lru_ba, lru_wx, lru_bx,
           lru_lam, hg_lb, hg_norm_g, w_proj_a, w_proj_b, w_out, norm2_g, w_rg, b_rg, w_re, b_re,
           w1, w3, w2, final_g):
    bsz, seq, d = x.shape
    seq_c = ctx.shape[1]
    l = 0

    pad = (-(bsz + 1)) % SUBLANES
    cc = jnp.concatenate([c, c_ctx[None, :], jnp.zeros((pad, d), F32)], axis=0)
    mod = _mod_call(cc, w_mod[l], b_mod[l])
    mx = [mod[:bsz, None, k * d:(k + 1) * d] for k in range(6)]
    mc = [jnp.broadcast_to(mod[bsz:bsz + 1, None, k * d:(k + 1) * d], (bsz, 1, d)) for k in range(2)]
    lb = jax.nn.softmax(hg_lb.astype(F32), axis=0)[l]

    hx = _norm_call(x, norm1_g[l], mx[0], mx[1])
    hc = _norm_call(ctx, norm1_g[l], mc[0], mc[1]).reshape(bsz * seq_c, d)
    rows = seq // GRID_W
    hx_cm = hx.reshape(bsz, rows, GRID_W, d).swapaxes(1, 2).reshape(bsz * seq, d)
    hx = hx.reshape(bsz * seq, d)
    z = _proj_call(hx, w_in[l], lambda j: jnp.where(j < 2, j, 7), 3, "proj_r")
    zh = _proj_call(hx_cm, w_in[l], lambda j: jnp.where(j < 5, j + 2, 8), 6, "proj_h")
    zc = _proj_call(hc, w_in[l], lambda j: jnp.where(j > 0, j + 2, 0), 4, "proj_c")

    ya = _lru_call(z, zc, conv_w[l], conv_b[l], lru_wa[l], lru_wx[l], lru_ba[l], lru_bx[l], lru_lam[l], bsz)
    s_f, s_b = _hgc_call(zc, lb, bsz)
    o_f = _hg_call(zh, 1, lb, s_f, bsz, reverse=False)
    mb_cm = _hg_call(zh, 2, lb, s_b, bsz, reverse=True, post=(o_f, hg_norm_g[l], w_proj_b[l]))
    mb = mb_cm.reshape(bsz, GRID_W, rows, d).swapaxes(1, 2).reshape(bsz, seq, d)

    w_router = jnp.zeros((d, LANES), F32).at[:, :N_EXPERTS].set(w_re[l])
    w_router = w_router.at[:, N_EXPERTS:N_EXPERTS + N_GROUPS].set(w_rg[l])
    b_router = jnp.zeros((1, LANES), F32).at[0, :N_EXPERTS].set(b_re[l])
    b_router = b_router.at[0, N_EXPERTS:N_EXPERTS + N_GROUPS].set(b_rg[l])
    x1, h2, route, cnt = _merge_call(x, ya, z, mb, mx[2], mx[3], mx[4], norm2_g[l],
                                     w_proj_a[l], w_out[l], w_router, b_router)

    route = route.reshape(bsz * seq, LANES)
    pos, tile_expert, n_used, zero_rows, zero_on, n_slots = _moe_plan(route, cnt, min(seq, 1024))
    xs = _dispatch_call(pos, zero_rows, zero_on, h2.reshape(bsz * seq, d), n_slots)
    ys = _experts_call(tile_expert, n_used, xs, w1[l], w3[l], w2[l])
    out = _combine_call(pos, route, x1.reshape(bsz * seq, d), mx[5], final_g, ys, seq)
    return out.reshape(bsz, seq, d)
```

```python
import functools

import jax
import jax.numpy as jnp
from jax import lax
from jax.experimental import pallas as pl
from jax.experimental.pallas import tpu as pltpu

F32 = jnp.float32
BF16 = jnp.bfloat16
HIGHEST = lax.Precision.HIGHEST

EPS = 1e-6
GRID_W = 64
CHUNK = 32
LRU_C = 8.0
CONV_W = 4
N_HEADS = 8
HEAD_DIM = 128
N_GROUPS = 4
EXP_PER_GROUP = 8
N_EXPERTS = N_GROUPS * EXP_PER_GROUP
LANES = 128
SUBLANES = 8
VMEM_LIMIT = 56 * 1024 * 1024


def _params(*sem):
    return pltpu.CompilerParams(dimension_semantics=sem, vmem_limit_bytes=VMEM_LIMIT)


def _sigmoid(x):
    return 0.5 * jnp.tanh(0.5 * x) + 0.5


def _silu(x):
    return x * _sigmoid(x)


def _bdot(a, b):
    return jnp.dot(a.astype(BF16), b.astype(BF16), preferred_element_type=F32)


def _mod_kernel(c_ref, w_ref, b_ref, o_ref):
    o_ref[...] = jnp.dot(_silu(c_ref[...]), w_ref[...], precision=HIGHEST,
                         preferred_element_type=F32) + b_ref[...]


def _mod_call(cc, w_mod, b_mod):
    rows, d = cc.shape
    n = w_mod.shape[1]
    tn = 1024
    return pl.pallas_call(
        _mod_kernel,
        grid=(n // tn,),
        in_specs=[pl.BlockSpec((rows, d), lambda j: (0, 0)),
                  pl.BlockSpec((d, tn), lambda j: (0, j)),
                  pl.BlockSpec((1, tn), lambda j: (0, j))],
        out_specs=pl.BlockSpec((rows, tn), lambda j: (0, j)),
        out_shape=jax.ShapeDtypeStruct((rows, n), F32),
        compiler_params=_params("arbitrary"),
        name="mod",
    )(cc, w_mod, b_mod.reshape(1, n))


def _norm_mod(x, g, shift, scale):
    y = x * lax.rsqrt(jnp.mean(x * x, axis=-1, keepdims=True) + EPS) * g
    return y * (1.0 + scale) + shift


def _norm_kernel(x_ref, g_ref, shift_ref, scale_ref, o_ref):
    o_ref[0] = _norm_mod(x_ref[0], g_ref[...], shift_ref[0], scale_ref[0]).astype(o_ref.dtype)


def _norm_call(x, g, shift, scale):
    b, l, d = x.shape
    tl = min(l, 512)
    vec = pl.BlockSpec((1, 1, d), lambda i, j: (i, 0, 0))
    return pl.pallas_call(
        _norm_kernel,
        grid=(b, l // tl),
        in_specs=[pl.BlockSpec((1, tl, d), lambda i, j: (i, j, 0)),
                  pl.BlockSpec((1, d), lambda i, j: (0, 0)), vec, vec],
        out_specs=pl.BlockSpec((1, tl, d), lambda i, j: (i, j, 0)),
        out_shape=jax.ShapeDtypeStruct((b, l, d), BF16),
        compiler_params=_params("arbitrary", "arbitrary"),
        name="norm",
    )(x, g.reshape(1, d), shift, scale)


def _proj_kernel(a_ref, w_ref, o_ref):
    o_ref[0] = jnp.dot(a_ref[...], w_ref[...].astype(BF16), preferred_element_type=F32)


def _proj_call(a, w, col_of, n_out, name):
    t, k = a.shape
    tn = 1024
    tm = min(t, 2048)
    return pl.pallas_call(
        _proj_kernel,
        grid=(t // tm, n_out),
        in_specs=[pl.BlockSpec((tm, k), lambda i, j: (i, 0)),
                  pl.BlockSpec((k, tn), lambda i, j: (0, col_of(j)))],
        out_specs=pl.BlockSpec((1, tm, tn), lambda i, j: (j, i, 0)),
        out_shape=jax.ShapeDtypeStruct((n_out, t, tn), F32),
        compiler_params=_params("arbitrary", "arbitrary"),
        name=name,
    )(a, w)


def _shift_rows(x, k):
    n = x.shape[0]
    row = lax.broadcasted_iota(jnp.int32, x.shape, 0)
    rolled = pltpu.roll(x, k % n, 0)
    keep = (row >= k) if k > 0 else (row < n + k)
    return jnp.where(keep, rolled, 0.0)


def _lru_kernel(ax_ref, axc_ref, cw_ref, cb_ref, wa_ref, wx_ref, ba_ref, bx_ref, lam_ref,
                o_ref, a_s, b_s):
    seq, width = ax_ref.shape[1], ax_ref.shape[2]
    seq_c = axc_ref.shape[1]
    heads = width // HEAD_DIM

    def conv(x):
        acc = _shift_rows(x, 2) * cw_ref[0:1, :]
        acc = acc + _shift_rows(x, 1) * cw_ref[1:2, :]
        acc = acc + x * cw_ref[2:3, :]
        acc = acc + _shift_rows(x, -1) * cw_ref[3:4, :]
        return acc + cb_ref[...]

    def coeffs(u, d, h):
        hs = slice(h * HEAD_DIM, (h + 1) * HEAD_DIM)
        ub = u.astype(BF16)
        r = _sigmoid(_bdot(ub, wa_ref[d, h]) + ba_ref[d:d + 1, hs])
        i = _sigmoid(_bdot(ub, wx_ref[d, h]) + bx_ref[d:d + 1, hs])
        log_a = r * (-LRU_C * jax.nn.softplus(-lam_ref[d:d + 1, hs]))
        a = jnp.exp(log_a)
        x = 1.0 - a * a
        return a, jnp.where(x > 0.0, x * lax.rsqrt(x), 0.0) * (i * u)

    u_c = conv(axc_ref[0])
    u_x = conv(ax_ref[0])
    n_blk = seq // SUBLANES
    row8 = lax.broadcasted_iota(jnp.int32, (SUBLANES, width), 0)

    for d, reverse in enumerate((False, True)):
        edge = u_c[seq_c - SUBLANES:] if reverse else u_c[:SUBLANES]
        pick = SUBLANES - 1 if reverse else 0
        h0 = jnp.concatenate([coeffs(edge[:, h * HEAD_DIM:(h + 1) * HEAD_DIM], d, h)[1][pick:pick + 1]
                              for h in range(heads)], axis=1)
        for h in range(heads):
            hs = slice(h * HEAD_DIM, (h + 1) * HEAD_DIM)
            a, b = coeffs(u_x[:, hs], d, h)
            a_s[:, hs] = a
            b_s[:, hs] = b

        def body(i, h_prev, reverse=reverse):
            blk = (n_blk - 1 - i) if reverse else i
            rows = pl.ds(pl.multiple_of(blk * SUBLANES, SUBLANES), SUBLANES)
            a = a_s[rows, :]
            b = b_s[rows, :]
            for s in (1, 2, 4):
                if reverse:
                    keep = row8 < SUBLANES - s
                    sh = SUBLANES - s
                else:
                    keep = row8 >= s
                    sh = s
                a_sh = jnp.where(keep, pltpu.roll(a, sh, 0), 1.0)
                b_sh = jnp.where(keep, pltpu.roll(b, sh, 0), 0.0)
                b = a * b_sh + b
                a = a * a_sh
            hh = b + a * h_prev
            if reverse:
                o_ref[0, rows, :] = o_ref[0, rows, :] + hh
                return hh[0:1, :]
            o_ref[0, rows, :] = hh
            return hh[SUBLANES - 1:SUBLANES, :]

        lax.fori_loop(0, n_blk, body, h0, unroll=8)


def _lru_call(z, zc, conv_w, conv_b, wa, wx, ba, bx, lam, bsz):
    t = z.shape[1]
    seq = t // bsz
    seq_c = zc.shape[1] // bsz
    width = 2 * HEAD_DIM
    n_w = (N_HEADS * HEAD_DIM) // width
    z4 = z.reshape(z.shape[0], bsz, seq, z.shape[2])
    zc4 = zc.reshape(zc.shape[0], bsz, seq_c, zc.shape[2])
    vec2 = pl.BlockSpec((2, width), lambda b, w: (0, w))
    wspec = pl.BlockSpec((2, width // HEAD_DIM, HEAD_DIM, HEAD_DIM), lambda b, w: (0, w, 0, 0))

    def kern(ax_ref, axc_ref, *rest):
        _lru_kernel(ax_ref.at[0], axc_ref.at[0], *rest)

    return pl.pallas_call(
        kern,
        grid=(bsz, n_w),
        in_specs=[pl.BlockSpec((1, 1, seq, width), lambda b, w: (0, b, 0, w)),
                  pl.BlockSpec((1, 1, seq_c, width), lambda b, w: (0, b, 0, w)),
                  pl.BlockSpec((CONV_W, width), lambda b, w: (0, w)),
                  pl.BlockSpec((1, width), lambda b, w: (0, w)),
                  wspec, wspec, vec2, vec2, vec2],
        out_specs=pl.BlockSpec((1, seq, width), lambda b, w: (b, 0, w)),
        out_shape=jax.ShapeDtypeStruct((bsz, seq, N_HEADS * HEAD_DIM), F32),
        scratch_shapes=[pltpu.VMEM((seq, width), F32), pltpu.VMEM((seq, width), F32)],
        compiler_params=_params("arbitrary", "arbitrary"),
        name="lru",
    )(z4, zc4, conv_w, conv_b.reshape(1, -1), wa, wx, ba, bx, lam)


def _split3(x):
    x1 = x.astype(BF16)
    r1 = x - x1.astype(F32)
    x2 = r1.astype(BF16)
    x3 = (r1 - x2.astype(F32)).astype(BF16)
    return x1, x2, x3


def _tri_cumsum(tri, x):
    x1, x2, x3 = _split3(x)
    dot = functools.partial(jnp.dot, preferred_element_type=F32)
    return dot(tri, x1) + dot(tri, x2) + dot(tri, x3)


def _hg_gates(f, lb):
    g = lb + (1.0 - lb) * _sigmoid(f)
    return jnp.log(g), 1.0 - g


def _dot_tn(a, b):
    return lax.dot_general(a.astype(BF16), b.astype(BF16), (((0,), (0,)), ((), ())),
                           preferred_element_type=F32)


def _dot_nt(a, b):
    return lax.dot_general(a.astype(BF16), b.astype(BF16), (((1,), (1,)), ((), ())),
                           preferred_element_type=F32)


def _hgc_kernel(ff_ref, fb_ref, v_ref, lb_ref, sf_ref, sb_ref):
    n = ff_ref.shape[1]
    row = lax.broadcasted_iota(jnp.int32, (n, n), 0)
    col = lax.broadcasted_iota(jnp.int32, (n, n), 1)
    tri = (col <= row).astype(BF16)
    lb = lb_ref[...]
    v = v_ref[0]
    lg, k = _hg_gates(ff_ref[0], lb)
    lc = _tri_cumsum(tri, lg)
    kf = k * jnp.exp(lc[n - 1:n, :] - lc)
    lg, k = _hg_gates(fb_ref[0], lb)
    lc = _tri_cumsum(tri, lg)
    kb = k * jnp.exp(lc - lg)
    for h in range(N_HEADS):
        hs = slice(h * HEAD_DIM, (h + 1) * HEAD_DIM)
        sf_ref[0, h] = _dot_tn(v[:, hs], kf[:, hs])
        sb_ref[0, h] = _dot_tn(v[:, hs], kb[:, hs])


def _hgc_call(zc, lb, bsz):
    seq_c = zc.shape[1] // bsz
    width = zc.shape[2]
    zc4 = zc.reshape(zc.shape[0], bsz, seq_c, width)

    def slab(j):
        return pl.BlockSpec((1, 1, seq_c, width), lambda b: (j, b, 0, 0))

    def kern(ff_ref, fb_ref, v_ref, lb_ref, sf_ref, sb_ref):
        _hgc_kernel(ff_ref.at[0], fb_ref.at[0], v_ref.at[0], lb_ref, sf_ref, sb_ref)

    sspec = pl.BlockSpec((1, N_HEADS, HEAD_DIM, HEAD_DIM), lambda b: (b, 0, 0, 0))
    sshape = jax.ShapeDtypeStruct((bsz, N_HEADS, HEAD_DIM, HEAD_DIM), F32)
    return pl.pallas_call(
        kern,
        grid=(bsz,),
        in_specs=[slab(1), slab(2), slab(3), pl.BlockSpec((1, width), lambda b: (0, 0))],
        out_specs=[sspec, sspec],
        out_shape=[sshape, sshape],
        compiler_params=_params("arbitrary"),
        name="hgc",
    )(zc4, zc4, zc4, lb.reshape(1, width))


def _hg_kernel(q_ref, f_ref, v_ref, lb_ref, o_ref, st_s, *, reverse, n_chunks, post):
    width = N_HEADS * HEAD_DIM
    rows = n_chunks * CHUNK

    row = lax.broadcasted_iota(jnp.int32, (rows, rows), 0)
    col = lax.broadcasted_iota(jnp.int32, (rows, rows), 1)
    same = (row // CHUNK) == (col // CHUNK)
    causal = same & ((col >= row) if reverse else (col <= row))
    last, mid = (0, CHUNK - 1 - CHUNK // 2) if reverse else (CHUNK - 1, CHUNK // 2)

    def chunked(t):
        return t.reshape(n_chunks, CHUNK, width)

    def flat(t):
        return t.reshape(rows, width)

    def chunk_cumsum(x):
        pos = lax.broadcasted_iota(jnp.int32, x.shape, 0) % CHUNK
        s = 1
        while s < CHUNK:
            if reverse:
                x = x + jnp.where(pos < CHUNK - s, pltpu.roll(x, rows - s, 0), 0.0)
            else:
                x = x + jnp.where(pos >= s, pltpu.roll(x, s, 0), 0.0)
            s *= 2
        return x

    q = chunked(_silu(q_ref[0]))
    v = v_ref[0]
    lg, k = _hg_gates(f_ref[0], lb_ref[...])
    k = chunked(k)
    lc = chunked(chunk_cumsum(lg))
    ltot = lc[:, last:last + 1, :]
    lmid = lc[:, mid:mid + 1, :]
    e_tot = jnp.exp(ltot)
    q_at = q * jnp.exp(lc - lmid)
    k_at = k * jnp.exp(lmid - lc)
    q_in = flat(q_at * jnp.exp(lmid))
    k_st = flat(k_at * jnp.exp(ltot - lmid))
    q_at = flat(q_at)
    k_at = flat(k_at)

    heads = [slice(h * HEAD_DIM, (h + 1) * HEAD_DIM) for h in range(N_HEADS)]
    chunks = [slice(c * CHUNK, (c + 1) * CHUNK) for c in range(n_chunks)]
    order = list(reversed(range(n_chunks))) if reverse else list(range(n_chunks))
    att = [jnp.where(causal, _dot_nt(q_at[:, hs], k_at[:, hs]), 0.0) for hs in heads]
    o_intra = [_bdot(att[h], v[:, hs]) for h, hs in enumerate(heads)]
    d_st = [[_dot_tn(v[rs, hs], k_st[rs, hs]) for rs in chunks] for hs in heads]
    outs = []
    for h, hs in enumerate(heads):
        st = st_s[h]
        st_in = [None] * n_chunks
        for c in order:
            st_in[c] = st
            st = st * e_tot[c, :, hs] + d_st[h][c]
        st_s[h] = st
        o_inter = [_dot_nt(q_in[rs, hs], st_in[c]) for c, rs in enumerate(chunks)]
        outs.append(o_intra[h] + jnp.concatenate(o_inter, axis=0))

    if post is None:
        for h in range(N_HEADS):
            o_ref[0, :, h * HEAD_DIM:(h + 1) * HEAD_DIM] = outs[h]
        return
    of_ref, og_ref, gb_ref, hgn_ref, wpb_ref = post
    parts = []
    for h in range(N_HEADS):
        hs = slice(h * HEAD_DIM, (h + 1) * HEAD_DIM)
        o = of_ref[0, :, hs] + outs[h]
        parts.append(o * lax.rsqrt(jnp.mean(o * o, axis=-1, keepdims=True) + EPS) * hgn_ref[:, hs])
    yb = jnp.concatenate(parts, axis=1)
    o_ref[0] = _sigmoid(gb_ref[0]) * _bdot(yb * _silu(og_ref[0]), wpb_ref[...])


def _hg_call(zh, f_slab, lb, s0, bsz, reverse, post=None):
    t, width = zh.shape[1], zh.shape[2]
    seq = t // bsz
    assert seq // GRID_W == CHUNK
    n_chunks = 8
    rows = n_chunks * CHUNK
    n_grp = seq // rows
    z4 = zh.reshape(zh.shape[0], bsz, seq, width)

    def grp(g):
        return (n_grp - 1 - g) if reverse else g

    nb = 2 if bsz % 2 == 0 else 1

    def slab(j):
        return pl.BlockSpec((1, nb, rows, width), lambda b, g: (j, b, grp(g), 0))

    tok = pl.BlockSpec((nb, rows, width), lambda b, g: (b, grp(g), 0))
    row = pl.BlockSpec((1, width), lambda b, g: (0, 0))
    in_specs = [slab(0), slab(f_slab), slab(3), row,
                pl.BlockSpec((nb, N_HEADS, HEAD_DIM, HEAD_DIM), lambda b, g: (b, 0, 0, 0))]
    args = [z4, z4, z4, lb.reshape(1, width), s0]
    if post is not None:
        o_other, hg_norm_g, w_proj_b = post
        in_specs += [tok, slab(4), slab(5), row, pl.BlockSpec((width, width), lambda b, g: (0, 0))]
        args += [o_other, z4, z4, hg_norm_g.reshape(1, width), w_proj_b.astype(BF16)]

    def kern(q_ref, f_ref, v_ref, lb_ref, s0_ref, *rest):
        *extra, o_ref, st_s = rest

        @pl.when(pl.program_id(1) == 0)
        def _():
            st_s[...] = s0_ref[...]

        for bi in range(nb):
            one = pl.ds(bi, 1)
            post_refs = None
            if extra:
                of_ref, og_ref, gb_ref, hgn_ref, wpb_ref = extra
                post_refs = (of_ref.at[one], og_ref.at[0, one], gb_ref.at[0, one], hgn_ref, wpb_ref)
            _hg_kernel(q_ref.at[0, one], f_ref.at[0, one], v_ref.at[0, one], lb_ref, o_ref.at[one],
                       st_s.at[bi], reverse=reverse, n_chunks=n_chunks, post=post_refs)

    return pl.pallas_call(
        kern,
        grid=(bsz // nb, n_grp),
        in_specs=in_specs,
        out_specs=tok,
        out_shape=jax.ShapeDtypeStruct((bsz, seq, width), F32),
        scratch_shapes=[pltpu.VMEM((nb, N_HEADS, HEAD_DIM, HEAD_DIM), F32)],
        compiler_params=_params("arbitrary", "arbitrary"),
        name="hg_bwd" if reverse else "hg_fwd",
    )(*args)


def _merge_kernel(x_ref, ya_ref, ay_ref, ga_ref, mb_ref,
                  gate_ref, shift_ref, scale_ref, n2_ref,
                  wpa_ref, wo_ref, wr_ref, br_ref,
                  x1_ref, h2_ref, route_ref, cnt_ref, run_s):
    tm = x_ref.shape[1]

    @pl.when((pl.program_id(0) == 0) & (pl.program_id(1) == 0))
    def _():
        run_s[...] = jnp.zeros_like(run_s)

    pa = _bdot(ya_ref[0] * jax.nn.gelu(ay_ref[0, 0]), wpa_ref[...])
    merged = _sigmoid(ga_ref[0, 0]) * pa + mb_ref[0]
    x1 = x_ref[0] + gate_ref[0] * _bdot(merged, wo_ref[...])
    x1_ref[0] = x1
    h2 = _norm_mod(x1, n2_ref[...], shift_ref[0], scale_ref[0])
    h2_ref[0] = h2

    h_hi, h_lo, _ = _split3(h2)
    w_hi, w_lo, _ = _split3(wr_ref[...])
    dot = functools.partial(jnp.dot, preferred_element_type=F32)
    logit = dot(h_hi, w_hi) + (dot(h_hi, w_lo) + dot(h_lo, w_hi)) + br_ref[...]
    col = lax.broadcasted_iota(jnp.int32, (tm, LANES), 1)
    colf = col.astype(F32)
    neg = -jnp.inf

    def first_at(mask):
        return -jnp.max(jnp.where(mask, -colf, -1e9), axis=1, keepdims=True)

    is_g = (col >= N_EXPERTS) & (col < N_EXPERTS + N_GROUPS)
    lgm = jnp.where(is_g, logit, neg)
    mg = jnp.max(lgm, axis=1, keepdims=True)
    pg = 1.0 / jnp.sum(jnp.exp(lgm - mg), axis=1, keepdims=True)
    gi = first_at(is_g & (logit == mg)) - float(N_EXPERTS)
    is_e = (col < N_EXPERTS) & ((col // EXP_PER_GROUP).astype(F32) == gi)
    lem = jnp.where(is_e, logit, neg)
    ee = jnp.exp(lem - jnp.max(lem, axis=1, keepdims=True))
    pe = ee / jnp.sum(ee, axis=1, keepdims=True)
    p1 = jnp.max(jnp.where(is_e, pe, -1.0), axis=1, keepdims=True)
    i1 = first_at(is_e & (pe == p1))
    rest = is_e & (colf != i1)
    p2 = jnp.max(jnp.where(rest, pe, -1.0), axis=1, keepdims=True)
    i2 = first_at(rest & (pe == p2))
    den = p1 + p2

    r_i = lax.broadcasted_iota(jnp.int32, (tm, tm), 0)
    c_i = lax.broadcasted_iota(jnp.int32, (tm, tm), 1)
    earlier = (c_i < r_i).astype(BF16)
    oh1 = (colf == i1).astype(F32)
    oh2 = (colf == i2).astype(F32)
    run = run_s[...]
    rank1 = jnp.sum(oh1 * (_bdot(earlier, oh1) + run), axis=1, keepdims=True)
    run = run + jnp.sum(oh1, axis=0, keepdims=True)
    rank2 = jnp.sum(oh2 * (_bdot(earlier, oh2) + run), axis=1, keepdims=True)
    run = run + jnp.sum(oh2, axis=0, keepdims=True)
    run_s[...] = run
    cnt_ref[...] = run
    vals = (i1, i2, pg * p1 / den, pg * p2 / den, rank1, rank2)
    route_ref[0] = sum(jnp.where(col == k, v, 0.0) for k, v in enumerate(vals))


def _merge_call(x, ya, z, mb, gate, shift, scale, norm2_g, w_proj_a, w_out, w_router, b_router):
    bsz, seq, d = x.shape
    tm = min(seq, 512)
    z4 = z.reshape(z.shape[0], bsz, seq, d)
    tok = pl.BlockSpec((1, tm, d), lambda b, i: (b, i, 0))

    def slab(j):
        return pl.BlockSpec((1, 1, tm, d), lambda b, i: (j, b, i, 0))

    vec = pl.BlockSpec((1, 1, d), lambda b, i: (b, 0, 0))
    row = pl.BlockSpec((1, d), lambda b, i: (0, 0))
    wsp = pl.BlockSpec((d, d), lambda b, i: (0, 0))
    return pl.pallas_call(
        _merge_kernel,
        grid=(bsz, seq // tm),
        in_specs=[tok, tok, slab(1), slab(2), tok,
                  vec, vec, vec, row, wsp, wsp,
                  pl.BlockSpec((d, LANES), lambda b, i: (0, 0)),
                  pl.BlockSpec((1, LANES), lambda b, i: (0, 0))],
        out_specs=[tok, tok, pl.BlockSpec((1, tm, LANES), lambda b, i: (b, i, 0)),
                   pl.BlockSpec((1, LANES), lambda b, i: (0, 0))],
        out_shape=[jax.ShapeDtypeStruct((bsz, seq, d), F32),
                   jax.ShapeDtypeStruct((bsz, seq, d), F32),
                   jax.ShapeDtypeStruct((bsz, seq, LANES), F32),
                   jax.ShapeDtypeStruct((1, LANES), F32)],
        scratch_shapes=[pltpu.VMEM((1, LANES), F32)],
        compiler_params=_params("arbitrary", "arbitrary"),
        name="merge",
    )(x, ya, z4, z4, mb, gate, shift, scale, norm2_g.reshape(1, d),
      w_proj_a.astype(BF16), w_out.astype(BF16), w_router, b_router)


MOE_TILE = 512


def _row_copy(src_ref, src_row, dst_ref, dst_row, sem):
    return pltpu.make_async_copy(src_ref.at[pl.ds(src_row, 1), :], dst_ref.at[pl.ds(dst_row, 1), :], sem)


ROW_DMA_UNROLL = 8


def _start_and_wait_rows(copies, n_rows):
    def start(r, carry):
        for k, cp in enumerate(copies(r)):
            cp.start(priority=k)
        return carry

    def wait(r, carry):
        for cp in copies(r):
            cp.wait()
        return carry

    lax.fori_loop(0, n_rows, start, 0, unroll=ROW_DMA_UNROLL)
    lax.fori_loop(0, n_rows, wait, 0, unroll=ROW_DMA_UNROLL)


def _dispatch_kernel(zrow_ref, zon_ref, pos_ref, h_ref, xs_ref, zero_s, sem, zsem):
    tm = h_ref.shape[0]

    @pl.when(pl.program_id(0) == 0)
    def _():
        zero_s[...] = jnp.zeros_like(zero_s)

        def fill(j):
            row0 = pl.multiple_of(zrow_ref[j], MOE_TILE)
            return pltpu.make_async_copy(zero_s, xs_ref.at[pl.ds(row0, MOE_TILE), :], zsem)

        for j in range(zrow_ref.shape[0]):
            @pl.when(zon_ref[j] == 1)
            def _(j=j):
                fill(j).start()

        for j in range(zrow_ref.shape[0]):
            @pl.when(zon_ref[j] == 1)
            def _(j=j):
                fill(j).wait()

    def copies(r):
        return [_row_copy(h_ref, r, xs_ref, pos_ref[0, 0, k * tm + r], sem) for k in range(2)]

    _start_and_wait_rows(copies, tm)


def _dispatch_call(pos, zero_rows, zero_on, h2, n_slots):
    t, d = h2.shape
    tm = pos.shape[2] // 2
    return pl.pallas_call(
        _dispatch_kernel,
        grid_spec=pltpu.PrefetchScalarGridSpec(
            num_scalar_prefetch=2,
            grid=(t // tm,),
            in_specs=[pl.BlockSpec((1, 1, 2 * tm), lambda i, zr, zo: (i, 0, 0), memory_space=pltpu.SMEM),
                      pl.BlockSpec((tm, d), lambda i, zr, zo: (i, 0))],
            out_specs=pl.BlockSpec(memory_space=pltpu.HBM),
            scratch_shapes=[pltpu.VMEM((MOE_TILE, d), F32), pltpu.SemaphoreType.DMA(()),
                            pltpu.SemaphoreType.DMA(())]),
        out_shape=jax.ShapeDtypeStruct((n_slots, d), F32),
        compiler_params=_params("arbitrary"),
        name="dispatch",
    )(zero_rows, zero_on, pos, h2)


def _experts_kernel(te_ref, nu_ref, xs_ref, w1_ref, w3_ref, w2_ref, ys_ref, w1_s, w3_s, w2_s):
    i = pl.program_id(0)

    @pl.when(i < nu_ref[0])
    def _():
        @pl.when((i == 0) | (te_ref[i] != te_ref[jnp.maximum(i - 1, 0)]))
        def _():
            w1_s[...] = w1_ref[0].astype(BF16)
            w3_s[...] = w3_ref[0].astype(BF16)
            w2_s[...] = w2_ref[0].astype(BF16)

        xb = xs_ref[...].astype(BF16)
        up = _silu(_bdot(xb, w1_s[...])) * _bdot(xb, w3_s[...])
        ys_ref[...] = _bdot(up, w2_s[...])

    @pl.when(i >= nu_ref[0])
    def _():
        ys_ref[...] = jnp.zeros_like(ys_ref)


def _experts_call(tile_expert, n_used, xs, w1, w3, w2):
    n_slots, d = xs.shape
    _, _, d_e = w1.shape
    n_tiles = n_slots // MOE_TILE
    tok = pl.BlockSpec((MOE_TILE, d), lambda i, te, nu: (jnp.minimum(i, nu[0] - 1), 0))
    return pl.pallas_call(
        _experts_kernel,
        grid_spec=pltpu.PrefetchScalarGridSpec(
            num_scalar_prefetch=2,
            grid=(n_tiles,),
            in_specs=[tok,
                      pl.BlockSpec((1, d, d_e), lambda i, te, nu: (te[i], 0, 0)),
                      pl.BlockSpec((1, d, d_e), lambda i, te, nu: (te[i], 0, 0)),
                      pl.BlockSpec((1, d_e, d), lambda i, te, nu: (te[i], 0, 0))],
            out_specs=pl.BlockSpec((MOE_TILE, d), lambda i, te, nu: (i, 0)),
            scratch_shapes=[pltpu.VMEM((d, d_e), BF16), pltpu.VMEM((d, d_e), BF16), pltpu.VMEM((d_e, d), BF16)]),
        out_shape=jax.ShapeDtypeStruct((n_slots, d), F32),
        compiler_params=_params("arbitrary"),
        name="experts",
    )(tile_expert, n_used, xs, w1, w3, w2)


def _combine_kernel(pos_ref, route_ref, x1_ref, gate_ref, fg_ref, ys_ref, o_ref, y1_s, y2_s, sem):
    tm = x1_ref.shape[0]

    def copies(r):
        return [_row_copy(ys_ref, pos_ref[0, 0, r], y1_s, r, sem),
                _row_copy(ys_ref, pos_ref[0, 0, tm + r], y2_s, r, sem)]

    _start_and_wait_rows(copies, tm)
    route = route_ref[...]
    y = route[:, 2:3] * y1_s[...] + route[:, 3:4] * y2_s[...]
    x2 = x1_ref[...] + gate_ref[0] * y
    o_ref[...] = x2 * lax.rsqrt(jnp.mean(x2 * x2, axis=-1, keepdims=True) + EPS) * fg_ref[...]


def _combine_call(pos, route, x1, gate, final_g, ys, seq):
    t, d = x1.shape
    tm = pos.shape[2] // 2
    per_b = seq // tm
    tok = pl.BlockSpec((tm, d), lambda i: (i, 0))
    return pl.pallas_call(
        _combine_kernel,
        grid=(t // tm,),
        in_specs=[pl.BlockSpec((1, 1, 2 * tm), lambda i: (i, 0, 0), memory_space=pltpu.SMEM),
                  pl.BlockSpec((tm, LANES), lambda i: (i, 0)), tok,
                  pl.BlockSpec((1, 1, d), lambda i: (i // per_b, 0, 0)),
                  pl.BlockSpec((1, d), lambda i: (0, 0)),
                  pl.BlockSpec(memory_space=pltpu.HBM)],
        out_specs=tok,
        out_shape=jax.ShapeDtypeStruct((t, d), F32),
        scratch_shapes=[pltpu.VMEM((tm, d), F32), pltpu.VMEM((tm, d), F32), pltpu.SemaphoreType.DMA(())],
        compiler_params=_params("arbitrary"),
        name="combine",
    )(pos, route, x1, gate, final_g.reshape(1, d), ys)


def _moe_plan(route, cnt, tm):
    t = route.shape[0]
    counts = cnt[0, :N_EXPERTS].astype(jnp.int32)
    n_tiles_e = (counts + MOE_TILE - 1) // MOE_TILE
    tile_end = jnp.cumsum(n_tiles_e)
    offsets = (tile_end - n_tiles_e) * MOE_TILE
    n_used = tile_end[-1]
    experts = jnp.arange(N_EXPERTS, dtype=jnp.int32)

    def slot(eid, rank):
        base = jnp.sum(jnp.where(eid.astype(jnp.int32)[:, None] == experts[None, :], offsets[None, :], 0), axis=1)
        return base + rank.astype(jnp.int32)

    pos1 = slot(route[:, 0], route[:, 4]).reshape(t // tm, tm)
    pos2 = slot(route[:, 1], route[:, 5]).reshape(t // tm, tm)
    pos = jnp.concatenate([pos1, pos2], axis=1)[:, None, :]
    n_tiles = (2 * t) // MOE_TILE + N_EXPERTS
    tiles = jnp.arange(n_tiles, dtype=jnp.int32)
    tile_expert = jnp.sum(jnp.minimum(tiles, n_used - 1)[:, None] >= tile_end[None, :], axis=1).astype(jnp.int32)
    has_pad = counts % MOE_TILE != 0
    spare = (2 * t) // MOE_TILE + experts
    zero_rows = jnp.concatenate([jnp.where(has_pad, (tile_end - 1) * MOE_TILE, 0), spare * MOE_TILE])
    zero_on = jnp.concatenate([has_pad, spare >= n_used]).astype(jnp.int32)
    return (pos, tile_expert, n_used.reshape(1).astype(jnp.int32), zero_rows.astype(jnp.int32), zero_on,
            n_tiles * MOE_TILE)


def kernel(x, c, ctx, c_ctx, w_mod, b_mod, norm1_g, w_in, conv_w, conv_b, lru_wa, lru_ba, lru_wx, lru_bx,
           lru_lam, hg_lb, hg_norm_g, w_proj_a, w_proj_b, w_out, norm2_g, w_rg, b_rg, w_re, b_re,
           w1, w3, w2, final_g):
    bsz, seq, d = x.shape
    seq_c = ctx.shape[1]
    l = 0

    pad = (-(bsz + 1)) % SUBLANES
    cc = jnp.concatenate([c, c_ctx[None, :], jnp.zeros((pad, d), F32)], axis=0)
    mod = _mod_call(cc, w_mod[l], b_mod[l])
    mx = [mod[:bsz, None, k * d:(k + 1) * d] for k in range(6)]
    mc = [jnp.broadcast_to(mod[bsz:bsz + 1, None, k * d:(k + 1) * d], (bsz, 1, d)) for k in range(2)]
    lb = jax.nn.softmax(hg_lb.astype(F32), axis=0)[l]

    hx = _norm_call(x, norm1_g[l], mx[0], mx[1])
    hc = _norm_call(ctx, norm1_g[l], mc[0], mc[1]).reshape(bsz * seq_c, d)
    rows = seq // GRID_W
    hx_cm = hx.reshape(bsz, rows, GRID_W, d).swapaxes(1, 2).reshape(bsz * seq, d)
    hx = hx.reshape(bsz * seq, d)
    z = _proj_call(hx, w_in[l], lambda j: jnp.where(j < 2, j, 7), 3, "proj_r")
    zh = _proj_call(hx_cm, w_in[l], lambda j: jnp.where(j < 5, j + 2, 8), 6, "proj_h")
    zc = _proj_call(hc, w_in[l], lambda j: jnp.where(j > 0, j + 2, 0), 4, "proj_c")

    ya = _lru_call(z, zc, conv_w[l], conv_b[l], lru_wa[l], lru_wx[l], lru_ba[l], lru_bx[l], lru_lam[l], bsz)
    s_f, s_b = _hgc_call(zc, lb, bsz)
    o_f = _hg_call(zh, 1, lb, s_f, bsz, reverse=False)
    mb_cm = _hg_call(zh, 2, lb, s_b, bsz, reverse=True, post=(o_f, hg_norm_g[l], w_proj_b[l]))
    mb = mb_cm.reshape(bsz, GRID_W, rows, d).swapaxes(1, 2).reshape(bsz, seq, d)

    w_router = jnp.zeros((d, LANES), F32).at[:, :N_EXPERTS].set(w_re[l])
    w_router = w_router.at[:, N_EXPERTS:N_EXPERTS + N_GROUPS].set(w_rg[l])
    b_router = jnp.zeros((1, LANES), F32).at[0, :N_EXPERTS].set(b_re[l])
    b_router = b_router.at[0, N_EXPERTS:N_EXPERTS + N_GROUPS].set(b_rg[l])
    x1, h2, route, cnt = _merge_call(x, ya, z, mb, mx[2], mx[3], mx[4], norm2_g[l],
                                     w_proj_a[l], w_out[l], w_router, b_router)

    route = route.reshape(bsz * seq, LANES)
    pos, tile_expert, n_used, zero_rows, zero_on, n_slots = _moe_plan(route, cnt, min(seq, 1024))
    xs = _dispatch_call(pos, zero_rows, zero_on, h2.reshape(bsz * seq, d), n_slots)
    ys = _experts_call(tile_expert, n_used, xs, w1[l], w3[l], w2[l])
    out = _combine_call(pos, route, x1.reshape(bsz * seq, d), mx[5], final_g, ys, seq)
    return out.reshape(bsz, seq, d)
```

```python
import functools

import jax
import jax.numpy as jnp
from jax import lax
from jax.experimental import pallas as pl
from jax.experimental.pallas import tpu as pltpu

F32 = jnp.float32
BF16 = jnp.bfloat16
HIGHEST = lax.Precision.HIGHEST

EPS = 1e-6
GRID_W = 64
CHUNK = 32
LRU_C = 8.0
CONV_W = 4
N_HEADS = 8
HEAD_DIM = 128
N_GROUPS = 4
EXP_PER_GROUP = 8
N_EXPERTS = N_GROUPS * EXP_PER_GROUP
LANES = 128
SUBLANES = 8
VMEM_LIMIT = 56 * 1024 * 1024


def _params(*sem):
    return pltpu.CompilerParams(dimension_semantics=sem, vmem_limit_bytes=VMEM_LIMIT)


def _sigmoid(x):
    return 0.5 * jnp.tanh(0.5 * x) + 0.5


def _silu(x):
    return x * _sigmoid(x)


def _bdot(a, b):
    return jnp.dot(a.astype(BF16), b.astype(BF16), preferred_element_type=F32)


def _mod_kernel(c_ref, w_ref, b_ref, o_ref):
    o_ref[...] = jnp.dot(_silu(c_ref[...]), w_ref[...], precision=HIGHEST,
                         preferred_element_type=F32) + b_ref[...]


def _mod_call(cc, w_mod, b_mod):
    rows, d = cc.shape
    n = w_mod.shape[1]
    tn = 1024
    return pl.pallas_call(
        _mod_kernel,
        grid=(n // tn,),
        in_specs=[pl.BlockSpec((rows, d), lambda j: (0, 0)),
                  pl.BlockSpec((d, tn), lambda j: (0, j)),
                  pl.BlockSpec((1, tn), lambda j: (0, j))],
        out_specs=pl.BlockSpec((rows, tn), lambda j: (0, j)),
        out_shape=jax.ShapeDtypeStruct((rows, n), F32),
        compiler_params=_params("arbitrary"),
        name="mod",
    )(cc, w_mod, b_mod.reshape(1, n))


def _norm_mod(x, g, shift, scale):
    y = x * lax.rsqrt(jnp.mean(x * x, axis=-1, keepdims=True) + EPS) * g
    return y * (1.0 + scale) + shift


def _norm_kernel(x_ref, g_ref, shift_ref, scale_ref, o_ref):
    o_ref[0] = _norm_mod(x_ref[0], g_ref[...], shift_ref[0], scale_ref[0]).astype(o_ref.dtype)


def _norm_call(x, g, shift, scale):
    b, l, d = x.shape
    tl = min(l, 1024)
    vec = pl.BlockSpec((1, 1, d), lambda i, j: (i, 0, 0))
    return pl.pallas_call(
        _norm_kernel,
        grid=(b, l // tl),
        in_specs=[pl.BlockSpec((1, tl, d), lambda i, j: (i, j, 0)),
                  pl.BlockSpec((1, d), lambda i, j: (0, 0)), vec, vec],
        out_specs=pl.BlockSpec((1, tl, d), lambda i, j: (i, j, 0)),
        out_shape=jax.ShapeDtypeStruct((b, l, d), BF16),
        compiler_params=_params("arbitrary", "arbitrary"),
        name="norm",
    )(x, g.reshape(1, d), shift, scale)


def _proj_kernel(a_ref, w_ref, o_ref):
    o_ref[0] = jnp.dot(a_ref[...], w_ref[...].astype(BF16), preferred_element_type=F32)


def _proj_call(a, w, col_of, n_out, name):
    t, k = a.shape
    tn = 1024
    tm = min(t, 2048)
    return pl.pallas_call(
        _proj_kernel,
        grid=(t // tm, n_out),
        in_specs=[pl.BlockSpec((tm, k), lambda i, j: (i, 0)),
                  pl.BlockSpec((k, tn), lambda i, j: (0, col_of(j)))],
        out_specs=pl.BlockSpec((1, tm, tn), lambda i, j: (j, i, 0)),
        out_shape=jax.ShapeDtypeStruct((n_out, t, tn), F32),
        compiler_params=_params("arbitrary", "arbitrary"),
        name=name,
    )(a, w)


def _shift_rows(x, k):
    n = x.shape[0]
    row = lax.broadcasted_iota(jnp.int32, x.shape, 0)
    rolled = pltpu.roll(x, k % n, 0)
    keep = (row >= k) if k > 0 else (row < n + k)
    return jnp.where(keep, rolled, 0.0)


def _lru_kernel(ax_ref, axc_ref, cw_ref, cb_ref, wa_ref, wx_ref, ba_ref, bx_ref, lam_ref,
                o_ref, a_s, b_s):
    seq, width = ax_ref.shape[1], ax_ref.shape[2]
    seq_c = axc_ref.shape[1]
    heads = width // HEAD_DIM

    def conv(x):
        acc = _shift_rows(x, 2) * cw_ref[0:1, :]
        acc = acc + _shift_rows(x, 1) * cw_ref[1:2, :]
        acc = acc + x * cw_ref[2:3, :]
        acc = acc + _shift_rows(x, -1) * cw_ref[3:4, :]
        return acc + cb_ref[...]

    def coeffs(u, d, h):
        hs = slice(h * HEAD_DIM, (h + 1) * HEAD_DIM)
        ub = u.astype(BF16)
        r = _sigmoid(_bdot(ub, wa_ref[d, h]) + ba_ref[d:d + 1, hs])
        i = _sigmoid(_bdot(ub, wx_ref[d, h]) + bx_ref[d:d + 1, hs])
        log_a = r * (-LRU_C * jax.nn.softplus(-lam_ref[d:d + 1, hs]))
        a = jnp.exp(log_a)
        x = 1.0 - a * a
        return a, jnp.where(x > 0.0, x * lax.rsqrt(x), 0.0) * (i * u)

    u_c = conv(axc_ref[0])
    u_x = conv(ax_ref[0])
    n_blk = seq // SUBLANES
    row8 = lax.broadcasted_iota(jnp.int32, (SUBLANES, width), 0)

    for d, reverse in enumerate((False, True)):
        edge = u_c[seq_c - SUBLANES:] if reverse else u_c[:SUBLANES]
        pick = SUBLANES - 1 if reverse else 0
        h0 = jnp.concatenate([coeffs(edge[:, h * HEAD_DIM:(h + 1) * HEAD_DIM], d, h)[1][pick:pick + 1]
                              for h in range(heads)], axis=1)
        for h in range(heads):
            hs = slice(h * HEAD_DIM, (h + 1) * HEAD_DIM)
            a, b = coeffs(u_x[:, hs], d, h)
            a_s[:, hs] = a
            b_s[:, hs] = b

        def body(i, h_prev, reverse=reverse):
            blk = (n_blk - 1 - i) if reverse else i
            rows = pl.ds(pl.multiple_of(blk * SUBLANES, SUBLANES), SUBLANES)
            a = a_s[rows, :]
            b = b_s[rows, :]
            for s in (1, 2, 4):
                if reverse:
                    keep = row8 < SUBLANES - s
                    sh = SUBLANES - s
                else:
                    keep = row8 >= s
                    sh = s
                a_sh = jnp.where(keep, pltpu.roll(a, sh, 0), 1.0)
                b_sh = jnp.where(keep, pltpu.roll(b, sh, 0), 0.0)
                b = a * b_sh + b
                a = a * a_sh
            hh = b + a * h_prev
            if reverse:
                o_ref[0, rows, :] = o_ref[0, rows, :] + hh
                return hh[0:1, :]
            o_ref[0, rows, :] = hh
            return hh[SUBLANES - 1:SUBLANES, :]

        lax.fori_loop(0, n_blk, body, h0, unroll=8)


def _lru_call(z, zc, conv_w, conv_b, wa, wx, ba, bx, lam, bsz):
    t = z.shape[1]
    seq = t // bsz
    seq_c = zc.shape[1] // bsz
    width = 4 * HEAD_DIM
    n_w = (N_HEADS * HEAD_DIM) // width
    z4 = z.reshape(z.shape[0], bsz, seq, z.shape[2])
    zc4 = zc.reshape(zc.shape[0], bsz, seq_c, zc.shape[2])
    vec2 = pl.BlockSpec((2, width), lambda b, w: (0, w))
    wspec = pl.BlockSpec((2, width // HEAD_DIM, HEAD_DIM, HEAD_DIM), lambda b, w: (0, w, 0, 0))

    def kern(ax_ref, axc_ref, *rest):
        _lru_kernel(ax_ref.at[0], axc_ref.at[0], *rest)

    return pl.pallas_call(
        kern,
        grid=(bsz, n_w),
        in_specs=[pl.BlockSpec((1, 1, seq, width), lambda b, w: (0, b, 0, w)),
                  pl.BlockSpec((1, 1, seq_c, width), lambda b, w: (0, b, 0, w)),
                  pl.BlockSpec((CONV_W, width), lambda b, w: (0, w)),
                  pl.BlockSpec((1, width), lambda b, w: (0, w)),
                  wspec, wspec, vec2, vec2, vec2],
        out_specs=pl.BlockSpec((1, seq, width), lambda b, w: (b, 0, w)),
        out_shape=jax.ShapeDtypeStruct((bsz, seq, N_HEADS * HEAD_DIM), F32),
        scratch_shapes=[pltpu.VMEM((seq, width), F32), pltpu.VMEM((seq, width), F32)],
        compiler_params=_params("arbitrary", "arbitrary"),
        name="lru",
    )(z4, zc4, conv_w, conv_b.reshape(1, -1), wa, wx, ba, bx, lam)


def _split3(x):
    x1 = x.astype(BF16)
    r1 = x - x1.astype(F32)
    x2 = r1.astype(BF16)
    x3 = (r1 - x2.astype(F32)).astype(BF16)
    return x1, x2, x3


def _tri_cumsum(tri, x):
    x1, x2, x3 = _split3(x)
    dot = functools.partial(jnp.dot, preferred_element_type=F32)
    return dot(tri, x1) + dot(tri, x2) + dot(tri, x3)


def _hg_gates(f, lb):
    g = lb + (1.0 - lb) * _sigmoid(f)
    return jnp.log(g), 1.0 - g


def _dot_tn(a, b):
    return lax.dot_general(a.astype(BF16), b.astype(BF16), (((0,), (0,)), ((), ())),
                           preferred_element_type=F32)


def _dot_nt(a, b):
    return lax.dot_general(a.astype(BF16), b.astype(BF16), (((1,), (1,)), ((), ())),
                           preferred_element_type=F32)


def _hgc_kernel(ff_ref, fb_ref, v_ref, lb_ref, sf_ref, sb_ref):
    n = ff_ref.shape[1]
    row = lax.broadcasted_iota(jnp.int32, (n, n), 0)
    col = lax.broadcasted_iota(jnp.int32, (n, n), 1)
    tri = (col <= row).astype(BF16)
    lb = lb_ref[...]
    v = v_ref[0]
    lg, k = _hg_gates(ff_ref[0], lb)
    lc = _tri_cumsum(tri, lg)
    kf = k * jnp.exp(lc[n - 1:n, :] - lc)
    lg, k = _hg_gates(fb_ref[0], lb)
    lc = _tri_cumsum(tri, lg)
    kb = k * jnp.exp(lc - lg)
    for h in range(N_HEADS):
        hs = slice(h * HEAD_DIM, (h + 1) * HEAD_DIM)
        sf_ref[0, h] = _dot_tn(v[:, hs], kf[:, hs])
        sb_ref[0, h] = _dot_tn(v[:, hs], kb[:, hs])


def _hgc_call(zc, lb, bsz):
    seq_c = zc.shape[1] // bsz
    width = zc.shape[2]
    zc4 = zc.reshape(zc.shape[0], bsz, seq_c, width)

    def slab(j):
        return pl.BlockSpec((1, 1, seq_c, width), lambda b: (j, b, 0, 0))

    def kern(ff_ref, fb_ref, v_ref, lb_ref, sf_ref, sb_ref):
        _hgc_kernel(ff_ref.at[0], fb_ref.at[0], v_ref.at[0], lb_ref, sf_ref, sb_ref)

    sspec = pl.BlockSpec((1, N_HEADS, HEAD_DIM, HEAD_DIM), lambda b: (b, 0, 0, 0))
    sshape = jax.ShapeDtypeStruct((bsz, N_HEADS, HEAD_DIM, HEAD_DIM), F32)
    return pl.pallas_call(
        kern,
        grid=(bsz,),
        in_specs=[slab(1), slab(2), slab(3), pl.BlockSpec((1, width), lambda b: (0, 0))],
        out_specs=[sspec, sspec],
        out_shape=[sshape, sshape],
        compiler_params=_params("arbitrary"),
        name="hgc",
    )(zc4, zc4, zc4, lb.reshape(1, width))


def _hg_kernel(q_ref, f_ref, v_ref, lb_ref, o_ref, st_s, *, reverse, n_chunks, post):
    width = N_HEADS * HEAD_DIM
    rows = n_chunks * CHUNK

    row = lax.broadcasted_iota(jnp.int32, (rows, rows), 0)
    col = lax.broadcasted_iota(jnp.int32, (rows, rows), 1)
    same = (row // CHUNK) == (col // CHUNK)
    causal = same & ((col >= row) if reverse else (col <= row))
    last, mid = (0, CHUNK - 1 - CHUNK // 2) if reverse else (CHUNK - 1, CHUNK // 2)

    def chunked(t):
        return t.reshape(n_chunks, CHUNK, width)

    def flat(t):
        return t.reshape(rows, width)

    def chunk_cumsum(x):
        pos = lax.broadcasted_iota(jnp.int32, x.shape, 0) % CHUNK
        s = 1
        while s < CHUNK:
            if reverse:
                x = x + jnp.where(pos < CHUNK - s, pltpu.roll(x, rows - s, 0), 0.0)
            else:
                x = x + jnp.where(pos >= s, pltpu.roll(x, s, 0), 0.0)
            s *= 2
        return x

    q = chunked(_silu(q_ref[0]))
    v = v_ref[0]
    lg, k = _hg_gates(f_ref[0], lb_ref[...])
    k = chunked(k)
    lc = chunked(chunk_cumsum(lg))
    ltot = lc[:, last:last + 1, :]
    lmid = lc[:, mid:mid + 1, :]
    e_tot = jnp.exp(ltot)
    q_at = q * jnp.exp(lc - lmid)
    k_at = k * jnp.exp(lmid - lc)
    q_in = flat(q_at * jnp.exp(lmid))
    k_st = flat(k_at * jnp.exp(ltot - lmid))
    q_at = flat(q_at)
    k_at = flat(k_at)

    heads = [slice(h * HEAD_DIM, (h + 1) * HEAD_DIM) for h in range(N_HEADS)]
    chunks = [slice(c * CHUNK, (c + 1) * CHUNK) for c in range(n_chunks)]
    order = list(reversed(range(n_chunks))) if reverse else list(range(n_chunks))
    att = [jnp.where(causal, _dot_nt(q_at[:, hs], k_at[:, hs]), 0.0) for hs in heads]
    o_intra = [_bdot(att[h], v[:, hs]) for h, hs in enumerate(heads)]
    d_st = [[_dot_tn(v[rs, hs], k_st[rs, hs]) for rs in chunks] for hs in heads]
    outs = []
    for h, hs in enumerate(heads):
        st = st_s[h]
        st_in = [None] * n_chunks
        for c in order:
            st_in[c] = st
            st = st * e_tot[c, :, hs] + d_st[h][c]
        st_s[h] = st
        o_inter = [_dot_nt(q_in[rs, hs], st_in[c]) for c, rs in enumerate(chunks)]
        outs.append(o_intra[h] + jnp.concatenate(o_inter, axis=0))

    if post is None:
        for h in range(N_HEADS):
            o_ref[0, :, h * HEAD_DIM:(h + 1) * HEAD_DIM] = outs[h]
        return
    of_ref, og_ref, gb_ref, hgn_ref, wpb_ref = post
    parts = []
    for h in range(N_HEADS):
        hs = slice(h * HEAD_DIM, (h + 1) * HEAD_DIM)
        o = of_ref[0, :, hs] + outs[h]
        parts.append(o * lax.rsqrt(jnp.mean(o * o, axis=-1, keepdims=True) + EPS) * hgn_ref[:, hs])
    yb = jnp.concatenate(parts, axis=1)
    o_ref[0] = _sigmoid(gb_ref[0]) * _bdot(yb * _silu(og_ref[0]), wpb_ref[...])


def _hg_call(zh, f_slab, lb, s0, bsz, reverse, post=None):
    t, width = zh.shape[1], zh.shape[2]
    seq = t // bsz
    assert seq // GRID_W == CHUNK
    n_chunks = 8
    rows = n_chunks * CHUNK
    n_grp = seq // rows
    z4 = zh.reshape(zh.shape[0], bsz, seq, width)

    def grp(g):
        return (n_grp - 1 - g) if reverse else g

    nb = 2 if bsz % 2 == 0 else 1

    def slab(j):
        return pl.BlockSpec((1, nb, rows, width), lambda b, g: (j, b, grp(g), 0))

    tok = pl.BlockSpec((nb, rows, width), lambda b, g: (b, grp(g), 0))
    row = pl.BlockSpec((1, width), lambda b, g: (0, 0))
    in_specs = [slab(0), slab(f_slab), slab(3), row,
                pl.BlockSpec((nb, N_HEADS, HEAD_DIM, HEAD_DIM), lambda b, g: (b, 0, 0, 0))]
    args = [z4, z4, z4, lb.reshape(1, width), s0]
    if post is not None:
        o_other, hg_norm_g, w_proj_b = post
        in_specs += [tok, slab(4), slab(5), row, pl.BlockSpec((width, width), lambda b, g: (0, 0))]
        args += [o_other, z4, z4, hg_norm_g.reshape(1, width), w_proj_b.astype(BF16)]

    def kern(q_ref, f_ref, v_ref, lb_ref, s0_ref, *rest):
        *extra, o_ref, st_s = rest

        @pl.when(pl.program_id(1) == 0)
        def _():
            st_s[...] = s0_ref[...]

        for bi in range(nb):
            one = pl.ds(bi, 1)
            post_refs = None
            if extra:
                of_ref, og_ref, gb_ref, hgn_ref, wpb_ref = extra
                post_refs = (of_ref.at[one], og_ref.at[0, one], gb_ref.at[0, one], hgn_ref, wpb_ref)
            _hg_kernel(q_ref.at[0, one], f_ref.at[0, one], v_ref.at[0, one], lb_ref, o_ref.at[one],
                       st_s.at[bi], reverse=reverse, n_chunks=n_chunks, post=post_refs)

    return pl.pallas_call(
        kern,
        grid=(bsz // nb, n_grp),
        in_specs=in_specs,
        out_specs=tok,
        out_shape=jax.ShapeDtypeStruct((bsz, seq, width), F32),
        scratch_shapes=[pltpu.VMEM((nb, N_HEADS, HEAD_DIM, HEAD_DIM), F32)],
        compiler_params=_params("arbitrary", "arbitrary"),
        name="hg_bwd" if reverse else "hg_fwd",
    )(*args)


def _merge_kernel(x_ref, ya_ref, ay_ref, ga_ref, mb_ref,
                  gate_ref, shift_ref, scale_ref, n2_ref,
                  wpa_ref, wo_ref, wr_ref, br_ref,
                  x1_ref, h2_ref, route_ref, cnt_ref, run_s):
    tm = x_ref.shape[1]

    @pl.when((pl.program_id(0) == 0) & (pl.program_id(1) == 0))
    def _():
        run_s[...] = jnp.zeros_like(run_s)

    pa = _bdot(ya_ref[0] * jax.nn.gelu(ay_ref[0, 0]), wpa_ref[...])
    merged = _sigmoid(ga_ref[0, 0]) * pa + mb_ref[0]
    x1 = x_ref[0] + gate_ref[0] * _bdot(merged, wo_ref[...])
    x1_ref[0] = x1
    h2 = _norm_mod(x1, n2_ref[...], shift_ref[0], scale_ref[0])
    h2_ref[0] = h2

    h_hi, h_lo, _ = _split3(h2)
    w_hi, w_lo, _ = _split3(wr_ref[...])
    dot = functools.partial(jnp.dot, preferred_element_type=F32)
    logit = dot(h_hi, w_hi) + (dot(h_hi, w_lo) + dot(h_lo, w_hi)) + br_ref[...]
    col = lax.broadcasted_iota(jnp.int32, (tm, LANES), 1)
    colf = col.astype(F32)
    neg = -jnp.inf

    def first_at(mask):
        return -jnp.max(jnp.where(mask, -colf, -1e9), axis=1, keepdims=True)

    is_g = (col >= N_EXPERTS) & (col < N_EXPERTS + N_GROUPS)
    lgm = jnp.where(is_g, logit, neg)
    mg = jnp.max(lgm, axis=1, keepdims=True)
    pg = 1.0 / jnp.sum(jnp.exp(lgm - mg), axis=1, keepdims=True)
    gi = first_at(is_g & (logit == mg)) - float(N_EXPERTS)
    is_e = (col < N_EXPERTS) & ((col // EXP_PER_GROUP).astype(F32) == gi)
    lem = jnp.where(is_e, logit, neg)
    ee = jnp.exp(lem - jnp.max(lem, axis=1, keepdims=True))
    pe = ee / jnp.sum(ee, axis=1, keepdims=True)
    p1 = jnp.max(jnp.where(is_e, pe, -1.0), axis=1, keepdims=True)
    i1 = first_at(is_e & (pe == p1))
    rest = is_e & (colf != i1)
    p2 = jnp.max(jnp.where(rest, pe, -1.0), axis=1, keepdims=True)
    i2 = first_at(rest & (pe == p2))
    den = p1 + p2

    r_i = lax.broadcasted_iota(jnp.int32, (tm, tm), 0)
    c_i = lax.broadcasted_iota(jnp.int32, (tm, tm), 1)
    earlier = (c_i < r_i).astype(BF16)
    oh1 = (colf == i1).astype(F32)
    oh2 = (colf == i2).astype(F32)
    run = run_s[...]
    rank1 = jnp.sum(oh1 * (_bdot(earlier, oh1) + run), axis=1, keepdims=True)
    run = run + jnp.sum(oh1, axis=0, keepdims=True)
    rank2 = jnp.sum(oh2 * (_bdot(earlier, oh2) + run), axis=1, keepdims=True)
    run = run + jnp.sum(oh2, axis=0, keepdims=True)
    run_s[...] = run
    cnt_ref[...] = run
    vals = (i1, i2, pg * p1 / den, pg * p2 / den, rank1, rank2)
    route_ref[0] = sum(jnp.where(col == k, v, 0.0) for k, v in enumerate(vals))


def _merge_call(x, ya, z, mb, gate, shift, scale, norm2_g, w_proj_a, w_out, w_router, b_router):
    bsz, seq, d = x.shape
    tm = min(seq, 512)
    z4 = z.reshape(z.shape[0], bsz, seq, d)
    tok = pl.BlockSpec((1, tm, d), lambda b, i: (b, i, 0))

    def slab(j):
        return pl.BlockSpec((1, 1, tm, d), lambda b, i: (j, b, i, 0))

    vec = pl.BlockSpec((1, 1, d), lambda b, i: (b, 0, 0))
    row = pl.BlockSpec((1, d), lambda b, i: (0, 0))
    wsp = pl.BlockSpec((d, d), lambda b, i: (0, 0))
    return pl.pallas_call(
        _merge_kernel,
        grid=(bsz, seq // tm),
        in_specs=[tok, tok, slab(1), slab(2), tok,
                  vec, vec, vec, row, wsp, wsp,
                  pl.BlockSpec((d, LANES), lambda b, i: (0, 0)),
                  pl.BlockSpec((1, LANES), lambda b, i: (0, 0))],
        out_specs=[tok, tok, pl.BlockSpec((1, tm, LANES), lambda b, i: (b, i, 0)),
                   pl.BlockSpec((1, LANES), lambda b, i: (0, 0))],
        out_shape=[jax.ShapeDtypeStruct((bsz, seq, d), F32),
                   jax.ShapeDtypeStruct((bsz, seq, d), F32),
                   jax.ShapeDtypeStruct((bsz, seq, LANES), F32),
                   jax.ShapeDtypeStruct((1, LANES), F32)],
        scratch_shapes=[pltpu.VMEM((1, LANES), F32)],
        compiler_params=_params("arbitrary", "arbitrary"),
        name="merge",
    )(x, ya, z4, z4, mb, gate, shift, scale, norm2_g.reshape(1, d),
      w_proj_a.astype(BF16), w_out.astype(BF16), w_router, b_router)


MOE_TILE = 512


def _row_copy(src_ref, src_row, dst_ref, dst_row, sem):
    return pltpu.make_async_copy(src_ref.at[pl.ds(src_row, 1), :], dst_ref.at[pl.ds(dst_row, 1), :], sem)


ROW_DMA_UNROLL = 8


def _start_and_wait_rows(copies, n_rows):
    def start(r, carry):
        for k, cp in enumerate(copies(r)):
            cp.start(priority=k)
        return carry

    def wait(r, carry):
        for cp in copies(r):
            cp.wait()
        return carry

    lax.fori_loop(0, n_rows, start, 0, unroll=ROW_DMA_UNROLL)
    lax.fori_loop(0, n_rows, wait, 0, unroll=ROW_DMA_UNROLL)


def _dispatch_kernel(zrow_ref, zon_ref, pos_ref, h_ref, xs_ref, zero_s, sem, zsem):
    tm = h_ref.shape[0]

    @pl.when(pl.program_id(0) == 0)
    def _():
        zero_s[...] = jnp.zeros_like(zero_s)

        def fill(j):
            row0 = pl.multiple_of(zrow_ref[j], MOE_TILE)
            return pltpu.make_async_copy(zero_s, xs_ref.at[pl.ds(row0, MOE_TILE), :], zsem)

        for j in range(zrow_ref.shape[0]):
            @pl.when(zon_ref[j] == 1)
            def _(j=j):
                fill(j).start()

        for j in range(zrow_ref.shape[0]):
            @pl.when(zon_ref[j] == 1)
            def _(j=j):
                fill(j).wait()

    def copies(r):
        return [_row_copy(h_ref, r, xs_ref, pos_ref[0, 0, k * tm + r], sem) for k in range(2)]

    _start_and_wait_rows(copies, tm)


def _dispatch_call(pos, zero_rows, zero_on, h2, n_slots):
    t, d = h2.shape
    tm = pos.shape[2] // 2
    return pl.pallas_call(
        _dispatch_kernel,
        grid_spec=pltpu.PrefetchScalarGridSpec(
            num_scalar_prefetch=2,
            grid=(t // tm,),
            in_specs=[pl.BlockSpec((1, 1, 2 * tm), lambda i, zr, zo: (i, 0, 0), memory_space=pltpu.SMEM),
                      pl.BlockSpec((tm, d), lambda i, zr, zo: (i, 0))],
            out_specs=pl.BlockSpec(memory_space=pltpu.HBM),
            scratch_shapes=[pltpu.VMEM((MOE_TILE, d), F32), pltpu.SemaphoreType.DMA(()),
                            pltpu.SemaphoreType.DMA(())]),
        out_shape=jax.ShapeDtypeStruct((n_slots, d), F32),
        compiler_params=_params("arbitrary"),
        name="dispatch",
    )(zero_rows, zero_on, pos, h2)


def _experts_kernel(te_ref, nu_ref, xs_ref, w1_ref, w3_ref, w2_ref, ys_ref, w1_s, w3_s, w2_s):
    i = pl.program_id(0)

    @pl.when(i < nu_ref[0])
    def _():
        @pl.when((i == 0) | (te_ref[i] != te_ref[jnp.maximum(i - 1, 0)]))
        def _():
            w1_s[...] = w1_ref[0].astype(BF16)
            w3_s[...] = w3_ref[0].astype(BF16)
            w2_s[...] = w2_ref[0].astype(BF16)

        xb = xs_ref[...].astype(BF16)
        up = _silu(_bdot(xb, w1_s[...])) * _bdot(xb, w3_s[...])
        ys_ref[...] = _bdot(up, w2_s[...])

    @pl.when(i >= nu_ref[0])
    def _():
        ys_ref[...] = jnp.zeros_like(ys_ref)


def _experts_call(tile_expert, n_used, xs, w1, w3, w2):
    n_slots, d = xs.shape
    _, _, d_e = w1.shape
    n_tiles = n_slots // MOE_TILE
    tok = pl.BlockSpec((MOE_TILE, d), lambda i, te, nu: (jnp.minimum(i, nu[0] - 1), 0))
    return pl.pallas_call(
        _experts_kernel,
        grid_spec=pltpu.PrefetchScalarGridSpec(
            num_scalar_prefetch=2,
            grid=(n_tiles,),
            in_specs=[tok,
                      pl.BlockSpec((1, d, d_e), lambda i, te, nu: (te[i], 0, 0)),
                      pl.BlockSpec((1, d, d_e), lambda i, te, nu: (te[i], 0, 0)),
                      pl.BlockSpec((1, d_e, d), lambda i, te, nu: (te[i], 0, 0))],
            out_specs=pl.BlockSpec((MOE_TILE, d), lambda i, te, nu: (i, 0)),
            scratch_shapes=[pltpu.VMEM((d, d_e), BF16), pltpu.VMEM((d, d_e), BF16), pltpu.VMEM((d_e, d), BF16)]),
        out_shape=jax.ShapeDtypeStruct((n_slots, d), F32),
        compiler_params=_params("arbitrary"),
        name="experts",
    )(tile_expert, n_used, xs, w1, w3, w2)


def _combine_kernel(pos_ref, route_ref, x1_ref, gate_ref, fg_ref, ys_ref, o_ref, y1_s, y2_s, sem):
    tm = x1_ref.shape[0]

    def copies(r):
        return [_row_copy(ys_ref, pos_ref[0, 0, r], y1_s, r, sem),
                _row_copy(ys_ref, pos_ref[0, 0, tm + r], y2_s, r, sem)]

    _start_and_wait_rows(copies, tm)
    route = route_ref[...]
    y = route[:, 2:3] * y1_s[...] + route[:, 3:4] * y2_s[...]
    x2 = x1_ref[...] + gate_ref[0] * y
    o_ref[...] = x2 * lax.rsqrt(jnp.mean(x2 * x2, axis=-1, keepdims=True) + EPS) * fg_ref[...]


def _combine_call(pos, route, x1, gate, final_g, ys, seq):
    t, d = x1.shape
    tm = pos.shape[2] // 2
    per_b = seq // tm
    tok = pl.BlockSpec((tm, d), lambda i: (i, 0))
    return pl.pallas_call(
        _combine_kernel,
        grid=(t // tm,),
        in_specs=[pl.BlockSpec((1, 1, 2 * tm), lambda i: (i, 0, 0), memory_space=pltpu.SMEM),
                  pl.BlockSpec((tm, LANES), lambda i: (i, 0)), tok,
                  pl.BlockSpec((1, 1, d), lambda i: (i // per_b, 0, 0)),
                  pl.BlockSpec((1, d), lambda i: (0, 0)),
                  pl.BlockSpec(memory_space=pltpu.HBM)],
        out_specs=tok,
        out_shape=jax.ShapeDtypeStruct((t, d), F32),
        scratch_shapes=[pltpu.VMEM((tm, d), F32), pltpu.VMEM((tm, d), F32), pltpu.SemaphoreType.DMA(())],
        compiler_params=_params("arbitrary"),
        name="combine",
    )(pos, route, x1, gate, final_g.reshape(1, d), ys)


def _moe_plan(route, cnt, tm):
    t = route.shape[0]
    counts = cnt[0, :N_EXPERTS].astype(jnp.int32)
    n_tiles_e = (counts + MOE_TILE - 1) // MOE_TILE
    tile_end = jnp.cumsum(n_tiles_e)
    offsets = (tile_end - n_tiles_e) * MOE_TILE
    n_used = tile_end[-1]
    experts = jnp.arange(N_EXPERTS, dtype=jnp.int32)

    def slot(eid, rank):
        base = jnp.sum(jnp.where(eid.astype(jnp.int32)[:, None] == experts[None, :], offsets[None, :], 0), axis=1)
        return base + rank.astype(jnp.int32)

    pos1 = slot(route[:, 0], route[:, 4]).reshape(t // tm, tm)
    pos2 = slot(route[:, 1], route[:, 5]).reshape(t // tm, tm)
    pos = jnp.concatenate([pos1, pos2], axis=1)[:, None, :]
    n_tiles = (2 * t) // MOE_TILE + N_EXPERTS
    tiles = jnp.arange(n_tiles, dtype=jnp.int32)
    tile_expert = jnp.sum(jnp.minimum(tiles, n_used - 1)[:, None] >= tile_end[None, :], axis=1).astype(jnp.int32)
    has_pad = counts % MOE_TILE != 0
    spare = (2 * t) // MOE_TILE + experts
    zero_rows = jnp.concatenate([jnp.where(has_pad, (tile_end - 1) * MOE_TILE, 0), spare * MOE_TILE])
    zero_on = jnp.concatenate([has_pad, spare >= n_used]).astype(jnp.int32)
    return (pos, tile_expert, n_used.reshape(1).astype(jnp.int32), zero_rows.astype(jnp.int32), zero_on,
            n_tiles * MOE_TILE)


def kernel(x, c, ctx, c_ctx, w_mod, b_mod, norm1_g, w_in, conv_w, conv_b, lru_wa, lru_ba, lru_wx, lru_bx,
           lru_lam, hg_lb, hg_norm_g, w_proj_a, w_proj_b, w_out, norm2_g, w_rg, b_rg, w_re, b_re,
           w1, w3, w2, final_g):
    bsz, seq, d = x.shape
    seq_c = ctx.shape[1]
    l = 0

    pad = (-(bsz + 1)) % SUBLANES
    cc = jnp.concatenate([c, c_ctx[None, :], jnp.zeros((pad, d), F32)], axis=0)
    mod = _mod_call(cc, w_mod[l], b_mod[l])
    mx = [mod[:bsz, None, k * d:(k + 1) * d] for k in range(6)]
    mc = [jnp.broadcast_to(mod[bsz:bsz + 1, None, k * d:(k + 1) * d], (bsz, 1, d)) for k in range(2)]
    lb = jax.nn.softmax(hg_lb.astype(F32), axis=0)[l]

    hx = _norm_call(x, norm1_g[l], mx[0], mx[1])
    hc = _norm_call(ctx, norm1_g[l], mc[0], mc[1]).reshape(bsz * seq_c, d)
    rows = seq // GRID_W
    hx_cm = hx.reshape(bsz, rows, GRID_W, d).swapaxes(1, 2).reshape(bsz * seq, d)
    hx = hx.reshape(bsz * seq, d)
    z = _proj_call(hx, w_in[l], lambda j: jnp.where(j < 2, j, 7), 3, "proj_r")
    zh = _proj_call(hx_cm, w_in[l], lambda j: jnp.where(j < 5, j + 2, 8), 6, "proj_h")
    zc = _proj_call(hc, w_in[l], lambda j: jnp.where(j > 0, j + 2, 0), 4, "proj_c")

    ya = _lru_call(z, zc, conv_w[l], conv_b[l], lru_wa[l], lru_wx[l], lru_ba[l], lru_bx[l], lru_lam[l], bsz)
    s_f, s_b = _hgc_call(zc, lb, bsz)
    o_f = _hg_call(zh, 1, lb, s_f, bsz, reverse=False)
    mb_cm = _hg_call(zh, 2, lb, s_b, bsz, reverse=True, post=(o_f, hg_norm_g[l], w_proj_b[l]))
    mb = mb_cm.reshape(bsz, GRID_W, rows, d).swapaxes(1, 2).reshape(bsz, seq, d)

    w_router = jnp.zeros((d, LANES), F32).at[:, :N_EXPERTS].set(w_re[l])
    w_router = w_router.at[:, N_EXPERTS:N_EXPERTS + N_GROUPS].set(w_rg[l])
    b_router = jnp.zeros((1, LANES), F32).at[0, :N_EXPERTS].set(b_re[l])
    b_router = b_router.at[0, N_EXPERTS:N_EXPERTS + N_GROUPS].set(b_rg[l])
    x1, h2, route, cnt = _merge_call(x, ya, z, mb, mx[2], mx[3], mx[4], norm2_g[l],
                                     w_proj_a[l], w_out[l], w_router, b_router)

    route = route.reshape(bsz * seq, LANES)
    pos, tile_expert, n_used, zero_rows, zero_on, n_slots = _moe_plan(route, cnt, min(seq, 1024))
    xs = _dispatch_call(pos, zero_rows, zero_on, h2.reshape(bsz * seq, d), n_slots)
    ys = _experts_call(tile_expert, n_used, xs, w1[l], w3[l], w2[l])
    out = _combine_call(pos, route, x1.reshape(bsz * seq, d), mx[5], final_g, ys, seq)
    return out.reshape(bsz, seq, d)
```
